```python
import jax, jax.numpy as jnp
from jax import lax
import numpy as np

D_MODEL = 1024
BATCH = 4
SEQ = 8192
DEPTH = 1

GRID_W = 64
CTX_LEN = 256
HEAD_DIM = 64
ATTN_HEADS = 8
ATTN_KV_HEADS = 2
GQA_GROUP = ATTN_HEADS // ATTN_KV_HEADS
RET_HEADS = 8
ATTN_WIDTH = ATTN_HEADS * HEAD_DIM
KV_WIDTH = ATTN_KV_HEADS * HEAD_DIM
RET_WIDTH = RET_HEADS * HEAD_DIM
MIX_WIDTH = ATTN_WIDTH + RET_WIDTH
IN_PROJ_WIDTH = ATTN_WIDTH + 2 * KV_WIDTH + 4 * RET_WIDTH
SPLITS = (ATTN_WIDTH, ATTN_WIDTH + KV_WIDTH, ATTN_WIDTH + 2 * KV_WIDTH,
          ATTN_WIDTH + 2 * KV_WIDTH + RET_WIDTH, ATTN_WIDTH + 2 * KV_WIDTH + 2 * RET_WIDTH,
          ATTN_WIDTH + 2 * KV_WIDTH + 3 * RET_WIDTH)
FFN_HIDDEN = -(-8 * D_MODEL // (3 * 256)) * 256
N_MOD = 6
QBLOCK = 128
RET_CHUNK = 128
ROPE_BASE = 10000.0
ATTN_SCALE = HEAD_DIM ** -0.5
EPS = 1e-6

kernel_name = 'hybrid_gqa_retention_dit_block'


def rms_norm(x, g):
    xf = x.astype(jnp.float32)
    y = xf * lax.rsqrt(jnp.mean(xf * xf, axis=-1, keepdims=True) + EPS)
    return (y * g.astype(jnp.float32)).astype(x.dtype)


def head_group_norm(o):
    mu = jnp.mean(o, axis=-1, keepdims=True)
    var = jnp.mean(jnp.square(o - mu), axis=-1, keepdims=True)
    return (o - mu) * lax.rsqrt(var + EPS)


def modulate(h, shift, scale):
    return h * (1 + scale) + shift


def axial_rope(rows):
    row = jnp.broadcast_to(jnp.arange(rows)[:, None], (rows, GRID_W)).reshape(-1).astype(jnp.float32)
    col = jnp.broadcast_to(jnp.arange(GRID_W)[None, :], (rows, GRID_W)).reshape(-1).astype(jnp.float32)
    n_freq = HEAD_DIM // 4
    inv = ROPE_BASE ** (-jnp.arange(n_freq, dtype=jnp.float32) / n_freq)
    ang = jnp.concatenate([row[:, None] * inv, col[:, None] * inv], axis=-1)
    return jnp.cos(ang), jnp.sin(ang)


def apply_rope(x, cos, sin):
    half = HEAD_DIM // 2
    x1, x2 = x[..., :half], x[..., half:]
    c, s = cos[None, :, None, :], sin[None, :, None, :]
    return jnp.concatenate([x1 * c - x2 * s, x1 * s + x2 * c], axis=-1).astype(x.dtype)


def project(h, w_in, q_norm_g, k_norm_g):
    B, N, _ = h.shape
    qa, ka, va, qr, kr, vr, gr = jnp.split(h @ w_in, SPLITS, axis=-1)
    heads = lambda a: a.reshape(B, N, -1, HEAD_DIM)
    qa = rms_norm(heads(qa), q_norm_g)
    ka = rms_norm(heads(ka), k_norm_g)
    return qa, ka, heads(va), heads(qr), heads(kr) * ATTN_SCALE, heads(vr), gr


def gqa_block(q, k, v):
    B, Q = q.shape[:2]
    qg = q.reshape(B, Q, ATTN_KV_HEADS, GQA_GROUP, HEAD_DIM)
    s = jnp.einsum('bqkgd,btkd->bkgqt', qg, k, preferred_element_type=jnp.float32) * ATTN_SCALE
    p = jax.nn.softmax(s, axis=-1).astype(v.dtype)
    o = jnp.einsum('bkgqt,btkd->bqkgd', p, v)
    return o.reshape(B, Q, ATTN_WIDTH)


def latent_attention(q, k_lat, v_lat, k_ctx, v_ctx):
    B, N = q.shape[:2]
    k_all = jnp.concatenate([k_ctx, k_lat], axis=1)
    v_all = jnp.concatenate([v_ctx, v_lat], axis=1)
    qb = q.reshape(B, N // QBLOCK, QBLOCK, ATTN_HEADS, HEAD_DIM).swapaxes(0, 1)
    o = lax.map(lambda qi: gqa_block(qi, k_all, v_all), qb)
    return o.swapaxes(0, 1).reshape(B, N, ATTN_WIDTH)


def retention_chunkwise(q, k, v, log_gamma, state0):
    B, N, H, d = q.shape
    C = RET_CHUNK
    nc = N // C
    lg = log_gamma.astype(jnp.float32)
    chunks = lambda a: a.astype(jnp.float32).reshape(B, nc, C, H, d).transpose(1, 0, 3, 2, 4)
    pos = jnp.arange(C, dtype=jnp.float32)
    diff = pos[:, None] - pos[None, :]
    decay_mat = jnp.where(diff >= 0, jnp.exp(lg[:, None, None] * jnp.maximum(diff, 0.0)), 0.0)
    q_decay = jnp.exp(lg[:, None] * (pos + 1.0))[..., None]
    k_decay = jnp.exp(lg[:, None] * (C - 1.0 - pos))[..., None]
    chunk_decay = jnp.exp(lg * C)[:, None, None]

    def step(state, qkv):
        qc, kc, vc = qkv
        inner = jnp.einsum('bhid,bhjd->bhij', qc, kc) * decay_mat
        o = jnp.einsum('bhij,bhjd->bhid', inner, vc) + jnp.einsum('bhid,bhde->bhie', qc, state) * q_decay
        state = state * chunk_decay + jnp.einsum('bhjd,bhje->bhde', kc * k_decay, vc)
        return state, o

    state, o = lax.scan(step, state0.astype(jnp.float32), (chunks(q), chunks(k), chunks(v)))
    return o.transpose(1, 0, 3, 2, 4).reshape(B, N, H, d), state


def bidir_retention(q, k, v, lg_f, lg_b, state_f, state_b):
    o_f, s_f = retention_chunkwise(q, k, v, lg_f, state_f)
    o_b, s_b = retention_chunkwise(q[:, ::-1], k[:, ::-1], v[:, ::-1], lg_b, state_b)
    return o_f + o_b[:, ::-1], s_f, s_b


def merge_heads(att, ret, g_r, w_out):
    B, N = att.shape[:2]
    ret = head_group_norm(ret).reshape(B, N, RET_WIDTH).astype(g_r.dtype) * jax.nn.silu(g_r)
    return jnp.concatenate([att, ret], axis=-1) @ w_out


def swiglu(h, w_ffn_in, w_ffn_out):
    a, b = jnp.split(h @ w_ffn_in, 2, axis=-1)
    return (jax.nn.silu(a) * b) @ w_ffn_out


def setup_inputs(seed: int = 0) -> dict:
    key = jax.random.key(seed)
    ks = jax.random.split(key, 18)
    D = D_MODEL
    nrm = lambda k, shape: jax.random.normal(k, shape, jnp.float32)
    w = lambda k, shape, fan_in, g=1.0: nrm(k, shape) * (g * fan_in ** -0.5)
    gain = lambda k, shape: 1.0 + 0.05 * nrm(k, shape)
    gammas = 1.0 - 2.0 ** (-5.0 - jnp.arange(RET_HEADS, dtype=jnp.float32))
    base = jnp.log(-jnp.log(gammas))
    return {
        'x': nrm(ks[0], (BATCH, SEQ, D)),
        'c': nrm(ks[1], (BATCH, D)),
        'ctx': nrm(ks[2], (BATCH, CTX_LEN, D)),
        'c_ctx': nrm(ks[3], (D,)),
        'w_mod': w(ks[4], (DEPTH, D, N_MOD * D), D, 0.5),
        'b_mod': 0.02 * nrm(ks[5], (DEPTH, N_MOD * D)),
        'g_pre_mix': gain(ks[6], (DEPTH, D)),
        'g_post_mix': gain(ks[7], (DEPTH, D)),
        'g_pre_ffn': gain(ks[8], (DEPTH, D)),
        'g_post_ffn': gain(ks[9], (DEPTH, D)),
        'w_in': w(ks[10], (DEPTH, D, IN_PROJ_WIDTH), D),
        'q_norm_g': gain(ks[11], (DEPTH, HEAD_DIM)),
        'k_norm_g': gain(ks[12], (DEPTH, HEAD_DIM)),
        'ret_decay_fwd': base + 0.1 * nrm(ks[13], (DEPTH, RET_HEADS)),
        'ret_decay_bwd': base + 0.1 * nrm(ks[14], (DEPTH, RET_HEADS)),
        'w_out': w(ks[15], (DEPTH, MIX_WIDTH, D), MIX_WIDTH),
        'w_ffn_in': w(ks[16], (DEPTH, D, 2 * FFN_HIDDEN), D),
        'w_ffn_out': w(ks[17], (DEPTH, FFN_HIDDEN, D), FFN_HIDDEN),
    }


def reference(x, c, ctx, c_ctx, w_mod, b_mod, g_pre_mix, g_post_mix, g_pre_ffn, g_post_ffn,
              w_in, q_norm_g, k_norm_g, ret_decay_fwd, ret_decay_bwd, w_out, w_ffn_in, w_ffn_out):
    B, n_lat, _ = x.shape
    rows = n_lat // GRID_W
    cos, sin = axial_rope(rows)
    for layer in range(DEPTH):
        mod_x = (jax.nn.silu(c) @ w_mod[layer] + b_mod[layer])[:, None, :]
        mod_c = jax.nn.silu(c_ctx) @ w_mod[layer] + b_mod[layer]
        sh_m, sc_m, gt_m, sh_f, sc_f, gt_f = jnp.split(mod_x, N_MOD, axis=-1)
        csh_m, csc_m, cgt_m, csh_f, csc_f, cgt_f = jnp.split(mod_c, N_MOD, axis=-1)
        lg_f = -jnp.exp(ret_decay_fwd[layer].astype(jnp.float32))
        lg_b = -jnp.exp(ret_decay_bwd[layer].astype(jnp.float32))

        hx = modulate(rms_norm(x, g_pre_mix[layer]), sh_m, sc_m)
        hc = modulate(rms_norm(ctx, g_pre_mix[layer]), csh_m, csc_m)
        qa_x, ka_x, va_x, qr_x, kr_x, vr_x, gr_x = project(hx, w_in[layer], q_norm_g[layer], k_norm_g[layer])
        qa_c, ka_c, va_c, qr_c, kr_c, vr_c, gr_c = project(hc, w_in[layer], q_norm_g[layer], k_norm_g[layer])

        zero_state = jnp.zeros((B, RET_HEADS, HEAD_DIM, HEAD_DIM), jnp.float32)
        ret_c, st_f, st_b = bidir_retention(qr_c, kr_c, vr_c, lg_f, lg_b, zero_state, zero_state)

        att_x = latent_attention(apply_rope(qa_x, cos, sin), apply_rope(ka_x, cos, sin), va_x, ka_c, va_c)
        ret_x, _, _ = bidir_retention(apply_rope(qr_x, cos, sin), apply_rope(kr_x, cos, sin), vr_x,
                                      lg_f, lg_b, st_f, st_b)
        mix_x = merge_heads(att_x, ret_x, gr_x, w_out[layer])
        x_new = x + gt_m * rms_norm(mix_x, g_post_mix[layer])
        hf = modulate(rms_norm(x_new, g_pre_ffn[layer]), sh_f, sc_f)
        x_new = x_new + gt_f * rms_norm(swiglu(hf, w_ffn_in[layer], w_ffn_out[layer]), g_post_ffn[layer])

        if layer < DEPTH - 1:
            att_c = gqa_block(qa_c, ka_c, va_c)
            mix_c = merge_heads(att_c, ret_c, gr_c, w_out[layer])
            ctx = ctx + cgt_m * rms_norm(mix_c, g_post_mix[layer])
            hfc = modulate(rms_norm(ctx, g_pre_ffn[layer]), csh_f, csc_f)
            ctx = ctx + cgt_f * rms_norm(swiglu(hfc, w_ffn_in[layer], w_ffn_out[layer]), g_post_ffn[layer])
        x = x_new
    return x
```

```python
import functools
import math

import numpy as np
import jax
import jax.numpy as jnp
from jax import lax
from jax.experimental import pallas as pl
from jax.experimental.pallas import tpu as pltpu

D_MODEL = 1024
HEAD_DIM = 64
HALF = HEAD_DIM // 2
GRID_W = 64
ATTN_HEADS = 8
ATTN_KV_HEADS = 2
RET_HEADS = 8
ATTN_WIDTH = ATTN_HEADS * HEAD_DIM
KV_WIDTH = ATTN_KV_HEADS * HEAD_DIM
RET_WIDTH = RET_HEADS * HEAD_DIM
FFN_HIDDEN = 2816
N_MOD = 6
RET_CHUNK = 128
ROPE_BASE = 10000.0
ATTN_SCALE = HEAD_DIM ** -0.5
EPS = 1e-6
LOG2E = math.log2(math.e)

LANES = 128
VMEM_LIMIT = 56 * 1024 * 1024

F32 = jnp.float32
BF16 = jnp.bfloat16

_QA0 = 0
_KA0 = ATTN_WIDTH
_VA0 = ATTN_WIDTH + KV_WIDTH
_QR0 = ATTN_WIDTH + 2 * KV_WIDTH
_KR0 = _QR0 + RET_WIDTH
_VR0 = _KR0 + RET_WIDTH
_GR0 = _VR0 + RET_WIDTH


def _pair_perm(n_heads):
    idx = []
    for p in range(n_heads // 2):
        for seg in range(4):
            head = 2 * p + (seg % 2)
            lo = (seg // 2) * HALF
            idx.extend(head * HEAD_DIM + lo + d for d in range(HALF))
    return np.asarray(idx, np.int32)


def _dup_perm(n_heads):
    idx = []
    for h in range(n_heads):
        for seg in range(4):
            lo = (seg // 2) * HALF
            idx.extend(h * HEAD_DIM + lo + d for d in range(HALF))
    return np.asarray(idx, np.int32)


_PAIR8 = _pair_perm(8)
_DUP2 = _dup_perm(ATTN_KV_HEADS)
_PAIR_DIM = (_PAIR8[:LANES] % HEAD_DIM).astype(np.int32)


def _dot(a, b):
    return jnp.dot(a, b, preferred_element_type=F32)


def _dot_nt(a, b):
    return lax.dot_general(a, b, (((1,), (1,)), ((), ())), preferred_element_type=F32)


def _dot_tn(a, b):
    return lax.dot_general(a, b, (((0,), (0,)), ((), ())), preferred_element_type=F32)


def _split_dot(a, b_bf16):
    hi = a.astype(BF16)
    lo = (a - hi.astype(F32)).astype(BF16)
    return _dot(hi, b_bf16) + _dot(lo, b_bf16)


def _silu(x):
    return x * (1.0 / (1.0 + jnp.exp(-x)))


def _lane_iota(shape):
    return lax.broadcasted_iota(jnp.int32, shape, len(shape) - 1)


def _mod_kernel(c_ref, w_ref, b_ref, o_ref):
    h = _silu(c_ref[...]).astype(BF16)
    o_ref[...] = _dot(h, w_ref[...].astype(BF16)) + b_ref[...]


def _mod_call(c_rows, w_mod, b_mod):
    rows = c_rows.shape[0]
    n_out = w_mod.shape[1]
    tn = D_MODEL
    return pl.pallas_call(
        _mod_kernel,
        out_shape=jax.ShapeDtypeStruct((rows, n_out), F32),
        grid=(n_out // tn,),
        in_specs=[
            pl.BlockSpec((rows, D_MODEL), lambda j: (0, 0)),
            pl.BlockSpec((D_MODEL, tn), lambda j: (0, j)),
            pl.BlockSpec((1, tn), lambda j: (0, j)),
        ],
        out_specs=pl.BlockSpec((rows, tn), lambda j: (0, j)),
        compiler_params=pltpu.CompilerParams(dimension_semantics=("arbitrary",),
                                             vmem_limit_bytes=VMEM_LIMIT),
        name="adaln_mod",
    )(c_rows, w_mod, b_mod.reshape(1, n_out))


def _rope(y, cos_t, sin_t):
    blocks = []
    for p in range(y.shape[1] // LANES):
        yp = y[:, p * LANES:(p + 1) * LANES]
        blocks.append(yp * cos_t + pltpu.roll(yp, LANES // 2, 1) * sin_t)
    return blocks[0] if len(blocks) == 1 else jnp.concatenate(blocks, axis=1)


def _proj_kernel(*refs, latent):
    if latent:
        (x_ref, mod_ref, g_ref, w_ref, gq_ref, gk_ref, seg_ref, cos_ref, sin_ref,
         qa_ref, ka_ref, va_ref, qr_ref, kr_ref, vr_ref, gr_ref) = refs
    else:
        (x_ref, mod_ref, g_ref, w_ref, gk_ref,
         ka_ref, va_ref, kr_ref, vr_ref) = refs

    x = x_ref[0]
    ms = jnp.mean(x * x, axis=-1, keepdims=True)
    shift = mod_ref[0, 0:1, :]
    scale = mod_ref[0, 1:2, :]
    h = (x * lax.rsqrt(ms + EPS) * g_ref[...]) * (1.0 + scale) + shift
    hb = h.astype(BF16)

    col = 0

    def proj(width):
        nonlocal col
        y = _dot(hb, w_ref[:, col:col + width])
        col += width
        return y

    if latent:
        cos_t = cos_ref[...]
        sin_t = sin_ref[...]
        rope = lambda y: _rope(y, cos_t, sin_t)
        yq = proj(ATTN_WIDTH)
        ssq = _split_dot(yq * yq, seg_ref[...])
        gq = jnp.concatenate([gq_ref[...]] * (ATTN_WIDTH // LANES), axis=1)
        qn = yq * lax.rsqrt(ssq * (1.0 / HEAD_DIM) + EPS) * gq
        qa_ref[0] = (rope(qn) * (ATTN_SCALE * LOG2E)).astype(BF16)
    else:
        rope = lambda y: y

    yk = proj(2 * KV_WIDTH)
    kblocks = []
    for kv in range(ATTN_KV_HEADS):
        ykv = yk[:, kv * LANES:(kv + 1) * LANES]
        ssq = 0.5 * jnp.sum(ykv * ykv, axis=-1, keepdims=True)
        kblocks.append(ykv * lax.rsqrt(ssq * (1.0 / HEAD_DIM) + EPS) * gk_ref[...])
    ka_ref[0] = rope(jnp.concatenate(kblocks, axis=1)).astype(BF16)

    yv = proj(2 * KV_WIDTH)
    ones_lane = (_lane_iota(yv.shape) % LANES) >= HEAD_DIM
    va_ref[0] = jnp.where(ones_lane, 1.0, yv).astype(BF16)

    if latent:
        qr_ref[0] = rope(proj(RET_WIDTH)).astype(BF16)
    kr_ref[0] = rope(proj(RET_WIDTH) * ATTN_SCALE).astype(BF16)
    vr_ref[0] = proj(RET_WIDTH).astype(BF16)
    if latent:
        gr_ref[0] = proj(RET_WIDTH)


def _proj_call(x, mod, g_pre, w, gq_tab, gk_tab, seg, cos_t, sin_t, *, latent, tm):
    bsz, n, _ = x.shape
    wcols = w.shape[1]
    tok = lambda width: pl.BlockSpec((1, tm, width), lambda b, i: (b, i, 0))
    full = lambda shape: pl.BlockSpec(shape, lambda b, i: (0,) * len(shape))
    mod_spec = pl.BlockSpec((1, N_MOD, D_MODEL), (lambda b, i: (b, 0, 0)) if latent
                            else (lambda b, i: (0, 0, 0)))
    if latent:
        in_specs = [tok(D_MODEL), mod_spec, full((1, D_MODEL)), full((D_MODEL, wcols)),
                    full((1, LANES)), full((1, LANES)), full((ATTN_WIDTH, ATTN_WIDTH)),
                    pl.BlockSpec((tm, LANES), lambda b, i: (i, 0)),
                    pl.BlockSpec((tm, LANES), lambda b, i: (i, 0))]
        args = (x, mod, g_pre, w, gq_tab, gk_tab, seg, cos_t, sin_t)
        widths = [(ATTN_WIDTH, BF16), (2 * KV_WIDTH, BF16), (2 * KV_WIDTH, BF16),
                  (RET_WIDTH, BF16), (RET_WIDTH, BF16), (RET_WIDTH, BF16), (RET_WIDTH, F32)]
    else:
        in_specs = [tok(D_MODEL), mod_spec, full((1, D_MODEL)), full((D_MODEL, wcols)),
                    full((1, LANES))]
        args = (x, mod, g_pre, w, gk_tab)
        widths = [(2 * KV_WIDTH, BF16), (2 * KV_WIDTH, BF16), (RET_WIDTH, BF16), (RET_WIDTH, BF16)]
    return pl.pallas_call(
        functools.partial(_proj_kernel, latent=latent),
        out_shape=[jax.ShapeDtypeStruct((bsz, n, wd), dt) for wd, dt in widths],
        grid=(bsz, n // tm),
        in_specs=in_specs,
        out_specs=[tok(wd) for wd, _ in widths],
        compiler_params=pltpu.CompilerParams(dimension_semantics=("arbitrary", "arbitrary"),
                                             vmem_limit_bytes=VMEM_LIMIT),
        name="in_proj_latent" if latent else "in_proj_ctx",
    )(*args)


def _attn_kernel(q_ref, kc_ref, vc_ref, kl_ref, vl_ref, o_ref, qm_ref, m_ref, acc_ref, *, tk):
    tq = q_ref.shape[1]
    n_lat = kl_ref.shape[1]
    n_heads = 2 * (q_ref.shape[2] // LANES)

    slot_b = (_lane_iota((tq, LANES)) // HALF) % 2
    for h in range(n_heads):
        qp = q_ref[0, :, (h // 2) * LANES:(h // 2 + 1) * LANES]
        qm_ref[h] = jnp.where(slot_b == (h % 2), qp, jnp.zeros_like(qp))
    m_ref[...] = jnp.full(m_ref.shape, -jnp.inf, F32)
    acc_ref[...] = jnp.zeros(acc_ref.shape, F32)

    def step(k, v):
        for h in range(n_heads):
            s = _dot_nt(qm_ref[h], k)
            m_old = m_ref[h]
            m_new = jnp.maximum(m_old, jnp.max(s, axis=-1, keepdims=True))
            alpha = jnp.exp2(m_old - m_new)
            p = jnp.exp2(s - m_new[:, :1]).astype(BF16)
            acc_ref[h] = alpha * acc_ref[h] + _dot(p, v)
            m_ref[h] = m_new

    step(kc_ref[0], vc_ref[0])

    def body(j, carry):
        start = pl.multiple_of(j * tk, tk)
        step(kl_ref[0, pl.ds(start, tk), :], vl_ref[0, pl.ds(start, tk), :])
        return carry

    lax.fori_loop(0, n_lat // tk, body, 0)

    low = _lane_iota((tq, LANES)) < HEAD_DIM
    for p in range(n_heads // 2):
        acc_a = acc_ref[2 * p]
        acc_b = acc_ref[2 * p + 1]
        oa = acc_a / pltpu.roll(acc_a, HEAD_DIM, 1)
        ob = pltpu.roll(acc_b, HEAD_DIM, 1) / acc_b
        o_ref[0, :, p * LANES:(p + 1) * LANES] = jnp.where(low, oa, ob).astype(o_ref.dtype)


def _attn_call(qa, ka_c, va_c, ka_l, va_l, *, tq, tk):
    bsz, n, _ = qa.shape
    n_ctx = ka_c.shape[1]
    gw = ATTN_WIDTH // ATTN_KV_HEADS
    n_heads = ATTN_HEADS // ATTN_KV_HEADS
    kv_spec = lambda rows: pl.BlockSpec((1, rows, LANES), lambda b, g, i: (b, 0, g))
    return pl.pallas_call(
        functools.partial(_attn_kernel, tk=tk),
        out_shape=jax.ShapeDtypeStruct((bsz, n, ATTN_WIDTH), BF16),
        grid=(bsz, ATTN_KV_HEADS, n // tq),
        in_specs=[pl.BlockSpec((1, tq, gw), lambda b, g, i: (b, i, g)),
                  kv_spec(n_ctx), kv_spec(n_ctx), kv_spec(n), kv_spec(n)],
        out_specs=pl.BlockSpec((1, tq, gw), lambda b, g, i: (b, i, g)),
        scratch_shapes=[pltpu.VMEM((n_heads, tq, LANES), BF16),
                        pltpu.VMEM((n_heads, tq, LANES), F32),
                        pltpu.VMEM((n_heads, tq, LANES), F32)],
        compiler_params=pltpu.CompilerParams(
            dimension_semantics=("arbitrary", "arbitrary", "arbitrary"),
            vmem_limit_bytes=VMEM_LIMIT),
        name="gqa_flash_attention",
    )(qa, ka_c, va_c, ka_l, va_l)


N_PAIRS = RET_HEADS // 2


def _ret_kernel(lg_head_ref, lg_std_ref, lg_pair_ref,
                qf_ref, kf_ref, vf_ref, qb_ref, kb_ref, vb_ref, kc_ref, vc_ref,
                of_ref, ob_ref,
                dmat_ref, qdec_ref, kdec_ref, cdec_ref, state_ref):
    C = RET_CHUNK
    n_ctx = kc_ref.shape[1]
    first_call = (pl.program_id(0) == 0) & (pl.program_id(1) == 0)

    row = lax.broadcasted_iota(jnp.int32, (C, C), 0).astype(F32)
    colv = lax.broadcasted_iota(jnp.int32, (C, C), 1).astype(F32)
    pos = row[:, :LANES]
    std_a = _lane_iota((C, LANES)) < HEAD_DIM
    pair_a = (_lane_iota((C, LANES)) // HALF) % 2 == 0
    row_head = (lax.broadcasted_iota(jnp.int32, (LANES, LANES), 0) // HALF) % 2
    col_head = lax.broadcasted_iota(jnp.int32, (LANES, LANES), 1) // HEAD_DIM
    same_head = row_head == col_head

    @pl.when(first_call)
    def _tables():
        for d in range(2):
            diff = (row - colv) if d == 0 else (colv - row)
            for h in range(RET_HEADS):
                lg = -jnp.exp(lg_head_ref[d, h])
                dmat_ref[d, h] = jnp.where(diff >= 0, jnp.exp(lg * jnp.maximum(diff, 0.0)), 0.0)
            for p in range(N_PAIRS):
                lg_std = -jnp.exp(lg_std_ref[d, p])
                lg_pair = -jnp.exp(lg_pair_ref[d, p])
                q_steps = (pos + 1.0) if d == 0 else (C - pos)
                k_steps = (C - 1.0 - pos) if d == 0 else pos
                qdec_ref[d, p] = jnp.exp(lg_std * q_steps)
                kdec_ref[d, p] = jnp.exp(lg_pair * k_steps)
                cdec_ref[d, p] = jnp.exp(lg_std * float(C))

    @pl.when(pl.program_id(1) == 0)
    def _seed_states():
        cpos = lax.broadcasted_iota(jnp.int32, (n_ctx, LANES), 0).astype(F32)
        for d in range(2):
            steps = (n_ctx - 1.0 - cpos) if d == 0 else cpos
            for p in range(N_PAIRS):
                lg_pair = -jnp.exp(lg_pair_ref[d, p])
                kc = kc_ref[0, :, p * LANES:(p + 1) * LANES].astype(F32) * jnp.exp(lg_pair * steps)
                vc = vc_ref[0, :, p * LANES:(p + 1) * LANES]
                state_ref[d, p] = jnp.where(same_head, _dot_tn(kc.astype(BF16), vc), 0.0)

    for d, (q_ref, k_ref, v_ref, o_ref) in enumerate(((qf_ref, kf_ref, vf_ref, of_ref),
                                                      (qb_ref, kb_ref, vb_ref, ob_ref))):
        for p in range(N_PAIRS):
            sl = slice(p * LANES, (p + 1) * LANES)
            q = q_ref[0, :, sl]
            k = k_ref[0, :, sl]
            v = v_ref[0, :, sl]
            zq = jnp.zeros_like(q)
            inner_a = _dot_nt(jnp.where(pair_a, q, zq), k) * dmat_ref[d, 2 * p]
            inner_b = _dot_nt(jnp.where(pair_a, zq, q), k) * dmat_ref[d, 2 * p + 1]
            zv = jnp.zeros_like(v)
            state = state_ref[d, p]
            o = (_dot(inner_a.astype(BF16), jnp.where(std_a, v, zv))
                 + _dot(inner_b.astype(BF16), jnp.where(std_a, zv, v))
                 + _dot(q, state.astype(BF16)) * qdec_ref[d, p])
            o_ref[0, :, sl] = o
            kd = (k.astype(F32) * kdec_ref[d, p]).astype(BF16)
            state_ref[d, p] = state * cdec_ref[d, p] + jnp.where(same_head, _dot_tn(kd, v), 0.0)


def _ret_call(lg_head, lg_std, lg_pair, qr, kr, vr, kr_c, vr_c):
    bsz, n, _ = qr.shape
    n_ctx = kr_c.shape[1]
    C = RET_CHUNK
    nc = n // C
    fwd = pl.BlockSpec((1, C, RET_WIDTH), lambda b, c: (b, c, 0))
    bwd = pl.BlockSpec((1, C, RET_WIDTH), lambda b, c: (b, nc - 1 - c, 0))
    ctx = pl.BlockSpec((1, n_ctx, RET_WIDTH), lambda b, c: (b, 0, 0))
    full = lambda a: pl.BlockSpec(a.shape, lambda b, c: (0,) * a.ndim)
    return pl.pallas_call(
        _ret_kernel,
        out_shape=[jax.ShapeDtypeStruct((bsz, n, RET_WIDTH), F32)] * 2,
        grid=(bsz, nc),
        in_specs=[full(lg_head), full(lg_std), full(lg_pair), fwd, fwd, fwd, bwd, bwd, bwd, ctx, ctx],
        out_specs=[fwd, bwd],
        scratch_shapes=[pltpu.VMEM((2, RET_HEADS, C, C), F32),
                        pltpu.VMEM((2, N_PAIRS, C, LANES), F32),
                        pltpu.VMEM((2, N_PAIRS, C, LANES), F32),
                        pltpu.VMEM((2, N_PAIRS, 1, LANES), F32),
                        pltpu.VMEM((2, N_PAIRS, LANES, LANES), F32)],
        compiler_params=pltpu.CompilerParams(dimension_semantics=("arbitrary", "arbitrary"),
                                             vmem_limit_bytes=VMEM_LIMIT),
        name="bidir_retention",
    )(lg_head, lg_std, lg_pair, qr, kr, vr, qr, kr, vr, kr_c, vr_c)


def _merge_ffn_kernel(x_ref, att_ref, of_ref, ob_ref, gr_ref, mod_ref, gpm_ref, gpf_ref, gqf_ref,
                      seg_ref, wo_ref, wi_ref, w2_ref, out_ref, *, n_chunks):
    x = x_ref[0]
    gate_m = mod_ref[0, 2:3, :]
    shift_f = mod_ref[0, 3:4, :]
    scale_f = mod_ref[0, 4:5, :]
    gate_f = mod_ref[0, 5:6, :]

    ret = of_ref[0] + ob_ref[0]
    seg = seg_ref[...]
    mu = _split_dot(ret, seg) * (1.0 / HEAD_DIM)
    cen = ret - mu
    var = _split_dot(cen * cen, seg) * (1.0 / HEAD_DIM)
    ret_n = cen * lax.rsqrt(var + EPS) * _silu(gr_ref[0])

    mix = (_dot(att_ref[0], wo_ref[:ATTN_WIDTH, :])
           + _dot(ret_n.astype(BF16), wo_ref[ATTN_WIDTH:, :]))
    ms = jnp.mean(mix * mix, axis=-1, keepdims=True)
    x1 = x + gate_m * (mix * lax.rsqrt(ms + EPS) * gpm_ref[...])

    ms1 = jnp.mean(x1 * x1, axis=-1, keepdims=True)
    hf = ((x1 * lax.rsqrt(ms1 + EPS) * gpf_ref[...]) * (1.0 + scale_f) + shift_f).astype(BF16)

    acc = jnp.zeros(x.shape, F32)
    for j in range(n_chunks):
        ab = _dot(hf, wi_ref[j])
        half = ab.shape[1] // 2
        act = (_silu(ab[:, :half]) * ab[:, half:]).astype(BF16)
        acc = acc + _dot(act, w2_ref[j])
    ms2 = jnp.mean(acc * acc, axis=-1, keepdims=True)
    out_ref[0] = x1 + gate_f * (acc * lax.rsqrt(ms2 + EPS) * gqf_ref[...])


def _merge_ffn_call(x, att, o_f, o_b, gr, mod, g_post_mix, g_pre_ffn, g_post_ffn, seg, w_out,
                    w_ffn_in, w_ffn_out, *, tm):
    bsz, n, _ = x.shape
    n_chunks = w_ffn_in.shape[0]
    tok = lambda width: pl.BlockSpec((1, tm, width), lambda b, i: (b, i, 0))
    full = lambda a: pl.BlockSpec(a.shape, lambda b, i: (0,) * a.ndim,
                                  pipeline_mode=pl.Buffered(1))
    return pl.pallas_call(
        functools.partial(_merge_ffn_kernel, n_chunks=n_chunks),
        out_shape=jax.ShapeDtypeStruct(x.shape, F32),
        grid=(bsz, n // tm),
        in_specs=[tok(D_MODEL), tok(ATTN_WIDTH), tok(RET_WIDTH), tok(RET_WIDTH), tok(RET_WIDTH),
                  pl.BlockSpec((1, N_MOD, D_MODEL), lambda b, i: (b, 0, 0)),
                  full(g_post_mix), full(g_pre_ffn), full(g_post_ffn), full(seg),
                  full(w_out), full(w_ffn_in), full(w_ffn_out)],
        out_specs=tok(D_MODEL),
        compiler_params=pltpu.CompilerParams(dimension_semantics=("arbitrary", "arbitrary"),
                                             vmem_limit_bytes=VMEM_LIMIT),
        name="merge_outproj_ffn",
    )(x, att, o_f, o_b, gr, mod, g_post_mix, g_pre_ffn, g_post_ffn, seg, w_out, w_ffn_in, w_ffn_out)


def _rope_tables(n_lat):
    rows = n_lat // GRID_W
    row = jnp.broadcast_to(jnp.arange(rows)[:, None], (rows, GRID_W)).reshape(-1).astype(F32)
    col = jnp.broadcast_to(jnp.arange(GRID_W)[None, :], (rows, GRID_W)).reshape(-1).astype(F32)
    n_freq = HEAD_DIM // 4
    inv = ROPE_BASE ** (-jnp.arange(n_freq, dtype=F32) / n_freq)
    ang = jnp.concatenate([row[:, None] * inv, col[:, None] * inv], axis=-1)
    cos, sin = jnp.cos(ang), jnp.sin(ang)
    return (jnp.concatenate([cos, cos, cos, cos], axis=-1),
            jnp.concatenate([-sin, -sin, sin, sin], axis=-1))


def _segment_matrix(width, same):
    i = np.arange(width)
    return jnp.asarray(same(i[:, None], i[None, :]), BF16)


def kernel(x, c, ctx, c_ctx, w_mod, b_mod, g_pre_mix, g_post_mix, g_pre_ffn, g_post_ffn,
           w_in, q_norm_g, k_norm_g, ret_decay_fwd, ret_decay_bwd, w_out, w_ffn_in, w_ffn_out):
    bsz, n_lat, _ = x.shape
    assert w_mod.shape[0] == 1, "single-layer block"
    layer = 0

    w = w_in[layer]
    zeros64 = jnp.zeros((D_MODEL, HEAD_DIM), w.dtype)
    w_va = jnp.concatenate([w[:, _VA0:_VA0 + HEAD_DIM], zeros64,
                            w[:, _VA0 + HEAD_DIM:_VA0 + 2 * HEAD_DIM], zeros64], axis=1)
    w_ka = w[:, _KA0 + _DUP2]
    w_kr = w[:, _KR0 + _PAIR8]
    w_vr = w[:, _VR0:_VR0 + RET_WIDTH]
    w_lat = jnp.concatenate([w[:, _QA0 + _PAIR8], w_ka, w_va, w[:, _QR0 + _PAIR8], w_kr, w_vr,
                             w[:, _GR0:_GR0 + RET_WIDTH]], axis=1).astype(BF16)
    w_ctx = jnp.concatenate([w_ka, w_va, w_kr, w_vr], axis=1).astype(BF16)
    gq_tab = q_norm_g[layer][_PAIR_DIM].reshape(1, LANES)
    gk_tab = k_norm_g[layer][_PAIR_DIM].reshape(1, LANES)
    seg_pair = _segment_matrix(ATTN_WIDTH, lambda i, j: (i // LANES == j // LANES)
                               & ((i // HALF) % 2 == (j // HALF) % 2))
    seg_std = _segment_matrix(RET_WIDTH, lambda i, j: i // HEAD_DIM == j // HEAD_DIM)
    cos_t, sin_t = _rope_tables(n_lat)

    decay = jnp.stack([ret_decay_fwd[layer], ret_decay_bwd[layer]]).astype(F32)
    lg_head = jnp.broadcast_to(decay[:, :, None, None], (2, RET_HEADS, 1, LANES))
    std_head = np.arange(RET_WIDTH) // HEAD_DIM
    lg_std = decay[:, std_head].reshape(2, N_PAIRS, 1, LANES)
    lg_pair = decay[:, _PAIR8 // HEAD_DIM].reshape(2, N_PAIRS, 1, LANES)

    n_chunks = FFN_HIDDEN // 256
    wf = w_ffn_in[layer]
    w_ffn_in_c = jnp.concatenate(
        [wf[:, :FFN_HIDDEN].reshape(D_MODEL, n_chunks, 256),
         wf[:, FFN_HIDDEN:].reshape(D_MODEL, n_chunks, 256)], axis=-1
    ).transpose(1, 0, 2).astype(BF16)
    w_ffn_out_c = w_ffn_out[layer].reshape(n_chunks, 256, D_MODEL).astype(BF16)
    w_out_b = w_out[layer].astype(BF16)

    c_rows = jnp.concatenate([c, c_ctx[None, :], jnp.zeros((8 - bsz - 1, D_MODEL), c.dtype)], axis=0)
    mod = _mod_call(c_rows, w_mod[layer], b_mod[layer]).reshape(8, N_MOD, D_MODEL)

    g_pre = g_pre_mix[layer].reshape(1, D_MODEL)
    qa, ka_l, va_l, qr, kr, vr, gr = _proj_call(x, mod, g_pre, w_lat, gq_tab, gk_tab, seg_pair,
                                                cos_t, sin_t, latent=True, tm=512)
    ka_c, va_c, kr_c, vr_c = _proj_call(ctx, mod[bsz:bsz + 1], g_pre, w_ctx, None, gk_tab, None,
                                        None, None, latent=False, tm=ctx.shape[1])

    att = _attn_call(qa, ka_c, va_c, ka_l, va_l, tq=256, tk=512)
    o_f, o_b = _ret_call(lg_head, lg_std, lg_pair, qr, kr, vr, kr_c, vr_c)

    return _merge_ffn_call(x, att, o_f, o_b, gr, mod,
                           g_post_mix[layer].reshape(1, D_MODEL),
                           g_pre_ffn[layer].reshape(1, D_MODEL),
                           g_post_ffn[layer].reshape(1, D_MODEL),
                           seg_std, w_out_b, w_ffn_in_c, w_ffn_out_c, tm=512)
```

```python
import functools
import math

import numpy as np
import jax
import jax.numpy as jnp
from jax import lax
from jax.experimental import pallas as pl
from jax.experimental.pallas import tpu as pltpu

D_MODEL = 1024
HEAD_DIM = 64
HALF = HEAD_DIM // 2
GRID_W = 64
ATTN_HEADS = 8
ATTN_KV_HEADS = 2
RET_HEADS = 8
ATTN_WIDTH = ATTN_HEADS * HEAD_DIM
KV_WIDTH = ATTN_KV_HEADS * HEAD_DIM
RET_WIDTH = RET_HEADS * HEAD_DIM
FFN_HIDDEN = 2816
N_MOD = 6
RET_CHUNK = 128
ROPE_BASE = 10000.0
ATTN_SCALE = HEAD_DIM ** -0.5
EPS = 1e-6
LOG2E = math.log2(math.e)

LANES = 128
VMEM_LIMIT = 56 * 1024 * 1024

F32 = jnp.float32
BF16 = jnp.bfloat16

_QA0 = 0
_KA0 = ATTN_WIDTH
_VA0 = ATTN_WIDTH + KV_WIDTH
_QR0 = ATTN_WIDTH + 2 * KV_WIDTH
_KR0 = _QR0 + RET_WIDTH
_VR0 = _KR0 + RET_WIDTH
_GR0 = _VR0 + RET_WIDTH


def _pair_perm(n_heads):
    idx = []
    for p in range(n_heads // 2):
        for seg in range(4):
            head = 2 * p + (seg % 2)
            lo = (seg // 2) * HALF
            idx.extend(head * HEAD_DIM + lo + d for d in range(HALF))
    return np.asarray(idx, np.int32)


def _dup_perm(n_heads):
    idx = []
    for h in range(n_heads):
        for seg in range(4):
            lo = (seg // 2) * HALF
            idx.extend(h * HEAD_DIM + lo + d for d in range(HALF))
    return np.asarray(idx, np.int32)


_PAIR8 = _pair_perm(8)
_DUP2 = _dup_perm(ATTN_KV_HEADS)
_PAIR_DIM = (_PAIR8[:LANES] % HEAD_DIM).astype(np.int32)


def _dot(a, b):
    return jnp.dot(a, b, preferred_element_type=F32)


def _dot_nt(a, b):
    return lax.dot_general(a, b, (((1,), (1,)), ((), ())), preferred_element_type=F32)


def _dot_tn(a, b):
    return lax.dot_general(a, b, (((0,), (0,)), ((), ())), preferred_element_type=F32)


def _split_dot(a, b_bf16):
    hi = a.astype(BF16)
    lo = (a - hi.astype(F32)).astype(BF16)
    return _dot(hi, b_bf16) + _dot(lo, b_bf16)


def _silu(x):
    return x * (1.0 / (1.0 + jnp.exp(-x)))


def _lane_iota(shape):
    return lax.broadcasted_iota(jnp.int32, shape, len(shape) - 1)


def _mod_kernel(c_ref, w_ref, b_ref, o_ref):
    h = _silu(c_ref[...]).astype(BF16)
    o_ref[...] = _dot(h, w_ref[...].astype(BF16)) + b_ref[...]


def _mod_call(c_rows, w_mod, b_mod):
    rows = c_rows.shape[0]
    n_out = w_mod.shape[1]
    tn = D_MODEL
    return pl.pallas_call(
        _mod_kernel,
        out_shape=jax.ShapeDtypeStruct((rows, n_out), F32),
        grid=(n_out // tn,),
        in_specs=[
            pl.BlockSpec((rows, D_MODEL), lambda j: (0, 0)),
            pl.BlockSpec((D_MODEL, tn), lambda j: (0, j)),
            pl.BlockSpec((1, tn), lambda j: (0, j)),
        ],
        out_specs=pl.BlockSpec((rows, tn), lambda j: (0, j)),
        compiler_params=pltpu.CompilerParams(dimension_semantics=("arbitrary",),
                                             vmem_limit_bytes=VMEM_LIMIT),
        name="adaln_mod",
    )(c_rows, w_mod, b_mod.reshape(1, n_out))


def _rope(y, cos_t, sin_t):
    blocks = []
    for p in range(y.shape[1] // LANES):
        yp = y[:, p * LANES:(p + 1) * LANES]
        blocks.append(yp * cos_t + pltpu.roll(yp, LANES // 2, 1) * sin_t)
    return blocks[0] if len(blocks) == 1 else jnp.concatenate(blocks, axis=1)


def _proj_kernel(*refs, latent):
    if latent:
        (x_ref, mod_ref, g_ref, w_ref, gq_ref, gk_ref, seg_ref, cos_ref, sin_ref,
         qa_ref, ka_ref, va_ref, qr_ref, kr_ref, vr_ref, gr_ref) = refs
    else:
        (x_ref, mod_ref, g_ref, w_ref, gk_ref,
         ka_ref, va_ref, kr_ref, vr_ref) = refs

    x = x_ref[0]
    ms = jnp.mean(x * x, axis=-1, keepdims=True)
    shift = mod_ref[0, 0:1, :]
    scale = mod_ref[0, 1:2, :]
    h = (x * lax.rsqrt(ms + EPS) * g_ref[...]) * (1.0 + scale) + shift
    hb = h.astype(BF16)

    col = 0

    def proj(width):
        nonlocal col
        y = _dot(hb, w_ref[:, col:col + width])
        col += width
        return y

    if latent:
        cos_t = cos_ref[...]
        sin_t = sin_ref[...]
        rope = lambda y: _rope(y, cos_t, sin_t)
        yq = proj(ATTN_WIDTH)
        ssq = _split_dot(yq * yq, seg_ref[...])
        gq = jnp.concatenate([gq_ref[...]] * (ATTN_WIDTH // LANES), axis=1)
        qn = yq * lax.rsqrt(ssq * (1.0 / HEAD_DIM) + EPS) * gq
        qa_ref[0] = (rope(qn) * (ATTN_SCALE * LOG2E)).T.astype(BF16)
    else:
        rope = lambda y: y

    yk = proj(2 * KV_WIDTH)
    slot_b = (_lane_iota((yk.shape[0], LANES)) // HALF) % 2 == 1
    kblocks = []
    for kv in range(ATTN_KV_HEADS):
        ykv = yk[:, kv * LANES:(kv + 1) * LANES]
        ssq = 0.5 * jnp.sum(ykv * ykv, axis=-1, keepdims=True)
        kn = rope(ykv * lax.rsqrt(ssq * (1.0 / HEAD_DIM) + EPS) * gk_ref[...])
        kblocks += [jnp.where(slot_b, 0.0, kn), jnp.where(slot_b, kn, 0.0)]
    ka_ref[0] = jnp.concatenate(kblocks, axis=1).astype(BF16)

    yv = proj(2 * KV_WIDTH)
    ones_lane = (_lane_iota(yv.shape) % LANES) >= HEAD_DIM
    va_ref[0, 0] = jnp.where(ones_lane, 1.0, yv).T.astype(BF16)

    if latent:
        qr_ref[0] = rope(proj(RET_WIDTH)).astype(BF16)
    kr_ref[0] = rope(proj(RET_WIDTH) * ATTN_SCALE).astype(BF16)
    vr_ref[0] = proj(RET_WIDTH).astype(BF16)
    if latent:
        gr_ref[0] = proj(RET_WIDTH)


def _proj_call(x, mod, g_pre, w, gq_tab, gk_tab, seg, cos_t, sin_t, *, latent, tm):
    bsz, n, _ = x.shape
    wcols = w.shape[1]
    tok = lambda width: pl.BlockSpec((1, tm, width), lambda b, i: (b, i, 0))
    full = lambda shape: pl.BlockSpec(shape, lambda b, i: (0,) * len(shape))
    mod_spec = pl.BlockSpec((1, N_MOD, D_MODEL), (lambda b, i: (b, 0, 0)) if latent
                            else (lambda b, i: (0, 0, 0)))
    if latent:
        in_specs = [tok(D_MODEL), mod_spec, full((1, D_MODEL)), full((D_MODEL, wcols)),
                    full((1, LANES)), full((1, LANES)), full((ATTN_WIDTH, ATTN_WIDTH)),
                    pl.BlockSpec((tm, LANES), lambda b, i: (i, 0)),
                    pl.BlockSpec((tm, LANES), lambda b, i: (i, 0))]
        args = (x, mod, g_pre, w, gq_tab, gk_tab, seg, cos_t, sin_t)
    else:
        in_specs = [tok(D_MODEL), mod_spec, full((1, D_MODEL)), full((D_MODEL, wcols)),
                    full((1, LANES))]
        args = (x, mod, g_pre, w, gk_tab)
    token_major = lambda width, dt: (jax.ShapeDtypeStruct((bsz, n, width), dt), tok(width))
    q_t = (jax.ShapeDtypeStruct((bsz, ATTN_WIDTH, n), BF16),
           pl.BlockSpec((1, ATTN_WIDTH, tm), lambda b, i: (b, 0, i)))
    k_ab = token_major(4 * KV_WIDTH, BF16)
    v_t = (jax.ShapeDtypeStruct((bsz, n // tm, 2 * KV_WIDTH, tm), BF16),
           pl.BlockSpec((1, 1, 2 * KV_WIDTH, tm), lambda b, i: (b, i, 0, 0)))
    ret = token_major(RET_WIDTH, BF16)
    outs = ([q_t, k_ab, v_t, ret, ret, ret, token_major(RET_WIDTH, F32)] if latent
            else [k_ab, v_t, ret, ret])
    return pl.pallas_call(
        functools.partial(_proj_kernel, latent=latent),
        out_shape=[o[0] for o in outs],
        grid=(bsz, n // tm),
        in_specs=in_specs,
        out_specs=[o[1] for o in outs],
        compiler_params=pltpu.CompilerParams(dimension_semantics=("arbitrary", "arbitrary"),
                                             vmem_limit_bytes=VMEM_LIMIT),
        name="in_proj_latent" if latent else "in_proj_ctx",
    )(*args)


def _attn_kernel(qt_ref, kc_ref, vc_ref, kl_ref, vl_ref, o_ref, m_ref, acc_ref, s_ref):
    n_heads = 2 * (qt_ref.shape[1] // LANES)

    m_ref[...] = jnp.full(m_ref.shape, -jnp.inf, F32)
    acc_ref[...] = jnp.zeros(acc_ref.shape, F32)

    n_tiles = kl_ref.shape[1]

    def scores(k_ab, h):
        q_t = qt_ref[0, (h // 2) * LANES:(h // 2 + 1) * LANES, :]
        return _dot(k_ab[:, (h % 2) * LANES:(h % 2 + 1) * LANES], q_t)

    def accumulate(h, s, v_t):
        m_old = m_ref[h]
        m_new = jnp.maximum(m_old, jnp.max(s, axis=0, keepdims=True))
        alpha = jnp.exp2(m_old - m_new)
        p = jnp.exp2(s - m_new).astype(BF16)
        acc_ref[h] = alpha * acc_ref[h] + _dot(v_t, p)
        m_ref[h] = m_new

    def step(s_first, k_ab, v_t, k_ab_after):
        s_next = s_first
        for h in range(n_heads):
            s_cur = s_next
            s_next = scores(k_ab, h + 1) if h + 1 < n_heads else scores(k_ab_after, 0)
            accumulate(h, s_cur, v_t)
        return s_next

    s_ref[...] = step(scores(kc_ref[0], 0), kc_ref[0], vc_ref[0, 0], kl_ref[0, 0])

    def body(j, carry):
        s_ref[...] = step(s_ref[...], kl_ref[0, j], vl_ref[0, j],
                          kl_ref[0, jnp.minimum(j + 1, n_tiles - 1)])
        return carry

    lax.fori_loop(0, n_tiles, body, 0)

    heads = []
    for h in range(n_heads):
        acc = acc_ref[h]
        heads.append(acc[:HEAD_DIM] / acc[HEAD_DIM:])
    o_ref[0] = jnp.concatenate(heads, axis=0).T.astype(o_ref.dtype)


def _attn_call(qa_t, ka_c, va_c, ka_l, va_l, *, tq):
    bsz, _, n = qa_t.shape
    n_ctx = ka_c.shape[1]
    n_tiles, tk = ka_l.shape[1], ka_l.shape[2]
    gw = ATTN_WIDTH // ATTN_KV_HEADS
    n_heads = ATTN_HEADS // ATTN_KV_HEADS
    return pl.pallas_call(
        _attn_kernel,
        out_shape=jax.ShapeDtypeStruct((bsz, n, ATTN_WIDTH), BF16),
        grid=(bsz, ATTN_KV_HEADS, n // tq),
        in_specs=[pl.BlockSpec((1, gw, tq), lambda b, g, i: (b, g, i)),
                  pl.BlockSpec((1, n_ctx, 2 * LANES), lambda b, g, i: (b, 0, g)),
                  pl.BlockSpec((1, 1, LANES, n_ctx), lambda b, g, i: (b, 0, g, 0)),
                  pl.BlockSpec((1, n_tiles, tk, 2 * LANES), lambda b, g, i: (b, 0, 0, g)),
                  pl.BlockSpec((1, n_tiles, LANES, tk), lambda b, g, i: (b, 0, g, 0))],
        out_specs=pl.BlockSpec((1, tq, gw), lambda b, g, i: (b, i, g)),
        scratch_shapes=[pltpu.VMEM((n_heads, 1, tq), F32),
                        pltpu.VMEM((n_heads, LANES, tq), F32),
                        pltpu.VMEM((tk, tq), F32)],
        compiler_params=pltpu.CompilerParams(
            dimension_semantics=("arbitrary", "arbitrary", "arbitrary"),
            vmem_limit_bytes=VMEM_LIMIT),
        name="gqa_flash_attention",
    )(qa_t, ka_c, va_c, ka_l, va_l)


N_PAIRS = RET_HEADS // 2


def _ret_kernel(lg_head_ref, lg_std_ref, lg_pair_ref,
                qf_ref, kf_ref, vf_ref, qb_ref, kb_ref, vb_ref, kc_ref, vc_ref,
                of_ref, ob_ref,
                dmat_ref, qdec_ref, kdec_ref, cdec_ref, state_ref):
    C = RET_CHUNK
    n_ctx = kc_ref.shape[1]
    first_call = (pl.program_id(0) == 0) & (pl.program_id(1) == 0)

    row = lax.broadcasted_iota(jnp.int32, (C, C), 0).astype(F32)
    colv = lax.broadcasted_iota(jnp.int32, (C, C), 1).astype(F32)
    pos = row[:, :LANES]
    std_a = _lane_iota((C, LANES)) < HEAD_DIM
    pair_a = (_lane_iota((C, LANES)) // HALF) % 2 == 0
    row_head = (lax.broadcasted_iota(jnp.int32, (LANES, LANES), 0) // HALF) % 2
    col_head = lax.broadcasted_iota(jnp.int32, (LANES, LANES), 1) // HEAD_DIM
    same_head = row_head == col_head

    @pl.when(first_call)
    def _tables():
        for d in range(2):
            diff = (row - colv) if d == 0 else (colv - row)
            for h in range(RET_HEADS):
                lg = -jnp.exp(lg_head_ref[d, h])
                dmat_ref[d, h] = jnp.where(diff >= 0, jnp.exp(lg * jnp.maximum(diff, 0.0)), 0.0)
            for p in range(N_PAIRS):
                lg_std = -jnp.exp(lg_std_ref[d, p])
                lg_pair = -jnp.exp(lg_pair_ref[d, p])
                q_steps = (pos + 1.0) if d == 0 else (C - pos)
                k_steps = (C - 1.0 - pos) if d == 0 else pos
                qdec_ref[d, p] = jnp.exp(lg_std * q_steps)
                kdec_ref[d, p] = jnp.exp(lg_pair * k_steps)
                cdec_ref[d, p] = jnp.exp(lg_std * float(C))

    @pl.when(pl.program_id(1) == 0)
    def _seed_states():
        cpos = lax.broadcasted_iota(jnp.int32, (n_ctx, LANES), 0).astype(F32)
        for d in range(2):
            steps = (n_ctx - 1.0 - cpos) if d == 0 else cpos
            for p in range(N_PAIRS):
                lg_pair = -jnp.exp(lg_pair_ref[d, p])
                kc = kc_ref[0, :, p * LANES:(p + 1) * LANES].astype(F32) * jnp.exp(lg_pair * steps)
                vc = vc_ref[0, :, p * LANES:(p + 1) * LANES]
                state_ref[d, p] = jnp.where(same_head, _dot_tn(kc.astype(BF16), vc), 0.0)

    for d, (q_ref, k_ref, v_ref, o_ref) in enumerate(((qf_ref, kf_ref, vf_ref, of_ref),
                                                      (qb_ref, kb_ref, vb_ref, ob_ref))):
        for p in range(N_PAIRS):
            sl = slice(p * LANES, (p + 1) * LANES)
            q = q_ref[0, :, sl]
            k = k_ref[0, :, sl]
            v = v_ref[0, :, sl]
            zq = jnp.zeros_like(q)
            inner_a = _dot_nt(jnp.where(pair_a, q, zq), k) * dmat_ref[d, 2 * p]
            inner_b = _dot_nt(jnp.where(pair_a, zq, q), k) * dmat_ref[d, 2 * p + 1]
            zv = jnp.zeros_like(v)
            state = state_ref[d, p]
            o = (_dot(inner_a.astype(BF16), jnp.where(std_a, v, zv))
                 + _dot(inner_b.astype(BF16), jnp.where(std_a, zv, v))
                 + _dot(q, state.astype(BF16)) * qdec_ref[d, p])
            o_ref[0, :, sl] = o
            kd = (k.astype(F32) * kdec_ref[d, p]).astype(BF16)
            state_ref[d, p] = state * cdec_ref[d, p] + jnp.where(same_head, _dot_tn(kd, v), 0.0)


def _ret_call(lg_head, lg_std, lg_pair, qr, kr, vr, kr_c, vr_c):
    bsz, n, _ = qr.shape
    n_ctx = kr_c.shape[1]
    C = RET_CHUNK
    nc = n // C
    fwd = pl.BlockSpec((1, C, RET_WIDTH), lambda b, c: (b, c, 0))
    bwd = pl.BlockSpec((1, C, RET_WIDTH), lambda b, c: (b, nc - 1 - c, 0))
    ctx = pl.BlockSpec((1, n_ctx, RET_WIDTH), lambda b, c: (b, 0, 0))
    full = lambda a: pl.BlockSpec(a.shape, lambda b, c: (0,) * a.ndim)
    return pl.pallas_call(
        _ret_kernel,
        out_shape=[jax.ShapeDtypeStruct((bsz, n, RET_WIDTH), F32)] * 2,
        grid=(bsz, nc),
        in_specs=[full(lg_head), full(lg_std), full(lg_pair), fwd, fwd, fwd, bwd, bwd, bwd, ctx, ctx],
        out_specs=[fwd, bwd],
        scratch_shapes=[pltpu.VMEM((2, RET_HEADS, C, C), F32),
                        pltpu.VMEM((2, N_PAIRS, C, LANES), F32),
                        pltpu.VMEM((2, N_PAIRS, C, LANES), F32),
                        pltpu.VMEM((2, N_PAIRS, 1, LANES), F32),
                        pltpu.VMEM((2, N_PAIRS, LANES, LANES), F32)],
        compiler_params=pltpu.CompilerParams(dimension_semantics=("arbitrary", "arbitrary"),
                                             vmem_limit_bytes=VMEM_LIMIT),
        name="bidir_retention",
    )(lg_head, lg_std, lg_pair, qr, kr, vr, qr, kr, vr, kr_c, vr_c)


def _merge_ffn_kernel(x_ref, att_ref, of_ref, ob_ref, gr_ref, mod_ref, gpm_ref, gpf_ref, gqf_ref,
                      seg_ref, wo_ref, wi_ref, w2_ref, out_ref, *, n_chunks):
    x = x_ref[0]
    gate_m = mod_ref[0, 2:3, :]
    shift_f = mod_ref[0, 3:4, :]
    scale_f = mod_ref[0, 4:5, :]
    gate_f = mod_ref[0, 5:6, :]

    ret = of_ref[0] + ob_ref[0]
    seg = seg_ref[...]
    mu = _split_dot(ret, seg) * (1.0 / HEAD_DIM)
    cen = ret - mu
    var = _split_dot(cen * cen, seg) * (1.0 / HEAD_DIM)
    ret_n = cen * lax.rsqrt(var + EPS) * _silu(gr_ref[0])

    mix = (_dot(att_ref[0], wo_ref[:ATTN_WIDTH, :])
           + _dot(ret_n.astype(BF16), wo_ref[ATTN_WIDTH:, :]))
    ms = jnp.mean(mix * mix, axis=-1, keepdims=True)
    x1 = x + gate_m * (mix * lax.rsqrt(ms + EPS) * gpm_ref[...])

    ms1 = jnp.mean(x1 * x1, axis=-1, keepdims=True)
    hf = ((x1 * lax.rsqrt(ms1 + EPS) * gpf_ref[...]) * (1.0 + scale_f) + shift_f).astype(BF16)

    acc = jnp.zeros(x.shape, F32)
    for j in range(n_chunks):
        ab = _dot(hf, wi_ref[j])
        half = ab.shape[1] // 2
        act = (_silu(ab[:, :half]) * ab[:, half:]).astype(BF16)
        acc = acc + _dot(act, w2_ref[j])
    ms2 = jnp.mean(acc * acc, axis=-1, keepdims=True)
    out_ref[0] = x1 + gate_f * (acc * lax.rsqrt(ms2 + EPS) * gqf_ref[...])


def _merge_ffn_call(x, att, o_f, o_b, gr, mod, g_post_mix, g_pre_ffn, g_post_ffn, seg, w_out,
                    w_ffn_in, w_ffn_out, *, tm):
    bsz, n, _ = x.shape
    n_chunks = w_ffn_in.shape[0]
    tok = lambda width: pl.BlockSpec((1, tm, width), lambda b, i: (b, i, 0))
    full = lambda a: pl.BlockSpec(a.shape, lambda b, i: (0,) * a.ndim,
                                  pipeline_mode=pl.Buffered(1))
    return pl.pallas_call(
        functools.partial(_merge_ffn_kernel, n_chunks=n_chunks),
        out_shape=jax.ShapeDtypeStruct(x.shape, F32),
        grid=(bsz, n // tm),
        in_specs=[tok(D_MODEL), tok(ATTN_WIDTH), tok(RET_WIDTH), tok(RET_WIDTH), tok(RET_WIDTH),
                  pl.BlockSpec((1, N_MOD, D_MODEL), lambda b, i: (b, 0, 0)),
                  full(g_post_mix), full(g_pre_ffn), full(g_post_ffn), full(seg),
                  full(w_out), full(w_ffn_in), full(w_ffn_out)],
        out_specs=tok(D_MODEL),
        compiler_params=pltpu.CompilerParams(dimension_semantics=("arbitrary", "arbitrary"),
                                             vmem_limit_bytes=VMEM_LIMIT),
        name="merge_outproj_ffn",
    )(x, att, o_f, o_b, gr, mod, g_post_mix, g_pre_ffn, g_post_ffn, seg, w_out, w_ffn_in, w_ffn_out)


def _rope_tables(n_lat):
    rows = n_lat // GRID_W
    row = jnp.broadcast_to(jnp.arange(rows)[:, None], (rows, GRID_W)).reshape(-1).astype(F32)
    col = jnp.broadcast_to(jnp.arange(GRID_W)[None, :], (rows, GRID_W)).reshape(-1).astype(F32)
    n_freq = HEAD_DIM // 4
    inv = ROPE_BASE ** (-jnp.arange(n_freq, dtype=F32) / n_freq)
    ang = jnp.concatenate([row[:, None] * inv, col[:, None] * inv], axis=-1)
    cos, sin = jnp.cos(ang), jnp.sin(ang)
    return (jnp.concatenate([cos, cos, cos, cos], axis=-1),
            jnp.concatenate([-sin, -sin, sin, sin], axis=-1))


def _segment_matrix(width, same):
    i = np.arange(width)
    return jnp.asarray(same(i[:, None], i[None, :]), BF16)


def kernel(x, c, ctx, c_ctx, w_mod, b_mod, g_pre_mix, g_post_mix, g_pre_ffn, g_post_ffn,
           w_in, q_norm_g, k_norm_g, ret_decay_fwd, ret_decay_bwd, w_out, w_ffn_in, w_ffn_out):
    bsz, n_lat, _ = x.shape
    assert w_mod.shape[0] == 1, "single-layer block"
    layer = 0

    w = w_in[layer]
    zeros64 = jnp.zeros((D_MODEL, HEAD_DIM), w.dtype)
    w_va = jnp.concatenate([w[:, _VA0:_VA0 + HEAD_DIM], zeros64,
                            w[:, _VA0 + HEAD_DIM:_VA0 + 2 * HEAD_DIM], zeros64], axis=1)
    w_ka = w[:, _KA0 + _DUP2]
    w_kr = w[:, _KR0 + _PAIR8]
    w_vr = w[:, _VR0:_VR0 + RET_WIDTH]
    w_lat = jnp.concatenate([w[:, _QA0 + _PAIR8], w_ka, w_va, w[:, _QR0 + _PAIR8], w_kr, w_vr,
                             w[:, _GR0:_GR0 + RET_WIDTH]], axis=1).astype(BF16)
    w_ctx = jnp.concatenate([w_ka, w_va, w_kr, w_vr], axis=1).astype(BF16)
    gq_tab = q_norm_g[layer][_PAIR_DIM].reshape(1, LANES)
    gk_tab = k_norm_g[layer][_PAIR_DIM].reshape(1, LANES)
    seg_pair = _segment_matrix(ATTN_WIDTH, lambda i, j: (i // LANES == j // LANES)
                               & ((i // HALF) % 2 == (j // HALF) % 2))
    seg_std = _segment_matrix(RET_WIDTH, lambda i, j: i // HEAD_DIM == j // HEAD_DIM)
    cos_t, sin_t = _rope_tables(n_lat)

    decay = jnp.stack([ret_decay_fwd[layer], ret_decay_bwd[layer]]).astype(F32)
    lg_head = jnp.broadcast_to(decay[:, :, None, None], (2, RET_HEADS, 1, LANES))
    std_head = np.arange(RET_WIDTH) // HEAD_DIM
    lg_std = decay[:, std_head].reshape(2, N_PAIRS, 1, LANES)
    lg_pair = decay[:, _PAIR8 // HEAD_DIM].reshape(2, N_PAIRS, 1, LANES)

    n_chunks = FFN_HIDDEN // 256
    wf = w_ffn_in[layer]
    w_ffn_in_c = jnp.concatenate(
        [wf[:, :FFN_HIDDEN].reshape(D_MODEL, n_chunks, 256),
         wf[:, FFN_HIDDEN:].reshape(D_MODEL, n_chunks, 256)], axis=-1
    ).transpose(1, 0, 2).astype(BF16)
    w_ffn_out_c = w_ffn_out[layer].reshape(n_chunks, 256, D_MODEL).astype(BF16)
    w_out_b = w_out[layer].astype(BF16)

    c_rows = jnp.concatenate([c, c_ctx[None, :], jnp.zeros((8 - bsz - 1, D_MODEL), c.dtype)], axis=0)
    mod = _mod_call(c_rows, w_mod[layer], b_mod[layer]).reshape(8, N_MOD, D_MODEL)

    g_pre = g_pre_mix[layer].reshape(1, D_MODEL)
    qa, ka_l, va_l, qr, kr, vr, gr = _proj_call(x, mod, g_pre, w_lat, gq_tab, gk_tab, seg_pair,
                                                cos_t, sin_t, latent=True, tm=512)
    ka_c, va_c, kr_c, vr_c = _proj_call(ctx, mod[bsz:bsz + 1], g_pre, w_ctx, None, gk_tab, None,
                                        None, None, latent=False, tm=ctx.shape[1])

    tk = va_l.shape[-1]
    att = _attn_call(qa, ka_c, va_c, ka_l.reshape(bsz, n_lat // tk, tk, 4 * KV_WIDTH), va_l, tq=256)
    o_f, o_b = _ret_call(lg_head, lg_std, lg_pair, qr, kr, vr, kr_c, vr_c)

    return _merge_ffn_call(x, att, o_f, o_b, gr, mod,
                           g_post_mix[layer].reshape(1, D_MODEL),
                           g_pre_ffn[layer].reshape(1, D_MODEL),
                           g_post_ffn[layer].reshape(1, D_MODEL),
                           seg_std, w_out_b, w_ffn_in_c, w_ffn_out_c, tm=512)
```

```python
import functools
import math

import numpy as np
import jax
import jax.numpy as jnp
from jax import lax
from jax.experimental import pallas as pl
from jax.experimental.pallas import tpu as pltpu

D_MODEL = 1024
HEAD_DIM = 64
HALF = HEAD_DIM // 2
GRID_W = 64
ATTN_HEADS = 8
ATTN_KV_HEADS = 2
RET_HEADS = 8
ATTN_WIDTH = ATTN_HEADS * HEAD_DIM
KV_WIDTH = ATTN_KV_HEADS * HEAD_DIM
RET_WIDTH = RET_HEADS * HEAD_DIM
FFN_HIDDEN = 2816
FFN_CHUNK = 256
N_MOD = 6
RET_CHUNK = 128
ROPE_BASE = 10000.0
ATTN_SCALE = HEAD_DIM ** -0.5
EPS = 1e-6
LOG2E = math.log2(math.e)

LANES = 128
VMEM_LIMIT = 56 * 1024 * 1024

F32 = jnp.float32
BF16 = jnp.bfloat16

_QA0 = 0
_KA0 = ATTN_WIDTH
_VA0 = ATTN_WIDTH + KV_WIDTH
_QR0 = ATTN_WIDTH + 2 * KV_WIDTH
_KR0 = _QR0 + RET_WIDTH
_VR0 = _KR0 + RET_WIDTH
_GR0 = _VR0 + RET_WIDTH


def _pair_perm(n_heads):
    idx = []
    for p in range(n_heads // 2):
        for seg in range(4):
            head = 2 * p + (seg % 2)
            lo = (seg // 2) * HALF
            idx.extend(head * HEAD_DIM + lo + d for d in range(HALF))
    return np.asarray(idx, np.int32)


def _dup_perm(n_heads):
    idx = []
    for h in range(n_heads):
        for seg in range(4):
            lo = (seg // 2) * HALF
            idx.extend(h * HEAD_DIM + lo + d for d in range(HALF))
    return np.asarray(idx, np.int32)


_PAIR8 = _pair_perm(8)
_DUP2 = _dup_perm(ATTN_KV_HEADS)


def _projection_layout():
    std = np.arange(RET_WIDTH, dtype=np.int32)
    zeros = np.full(HEAD_DIM, _GR0 + RET_WIDTH, np.int32)
    v0 = _VA0 + np.arange(HEAD_DIM, dtype=np.int32)
    groups = [("qa", _QA0 + _PAIR8), ("ka", _KA0 + _DUP2),
              ("va", np.concatenate([v0, zeros, v0 + HEAD_DIM, zeros])),
              ("qr", _QR0 + _PAIR8), ("kr", _KR0 + _PAIR8), ("vr", _VR0 + std), ("gr", _GR0 + std)]
    cols, start = {}, 0
    for name, idx in groups:
        cols[name] = (start, len(idx))
        start += len(idx)
    return np.concatenate([idx for _, idx in groups]), cols


_PROJ_GATHER, _PROJ_COLS = _projection_layout()
_PAIR_DIM = (_PAIR8[:LANES] % HEAD_DIM).astype(np.int32)


def _dot(a, b):
    return jnp.dot(a, b, preferred_element_type=F32)


def _dot_nt(a, b):
    return lax.dot_general(a, b, (((1,), (1,)), ((), ())), preferred_element_type=F32)


def _dot_tn(a, b):
    return lax.dot_general(a, b, (((0,), (0,)), ((), ())), preferred_element_type=F32)


def _split_dot(a, b_bf16):
    hi = a.astype(BF16)
    lo = (a - hi.astype(F32)).astype(BF16)
    return _dot(hi, b_bf16) + _dot(lo, b_bf16)


def _silu(x):
    return x * (1.0 / (1.0 + jnp.exp(-x)))


def _lane_iota(shape):
    return lax.broadcasted_iota(jnp.int32, shape, len(shape) - 1)


def _mod_kernel(c_ref, w_ref, b_ref, o_ref):
    h = _silu(c_ref[...]).astype(BF16)
    o_ref[...] = _dot(h, w_ref[...].astype(BF16)) + b_ref[...]


def _mod_call(c_rows, w_mod, b_mod):
    rows = c_rows.shape[0]
    n_out = w_mod.shape[1]
    tn = D_MODEL
    return pl.pallas_call(
        _mod_kernel,
        out_shape=jax.ShapeDtypeStruct((rows, n_out), F32),
        grid=(n_out // tn,),
        in_specs=[
            pl.BlockSpec((rows, D_MODEL), lambda j: (0, 0)),
            pl.BlockSpec((D_MODEL, tn), lambda j: (0, j)),
            pl.BlockSpec((1, tn), lambda j: (0, j)),
        ],
        out_specs=pl.BlockSpec((rows, tn), lambda j: (0, j)),
        compiler_params=pltpu.CompilerParams(dimension_semantics=("arbitrary",),
                                             vmem_limit_bytes=VMEM_LIMIT),
        name="adaln_mod",
    )(c_rows, w_mod, b_mod.reshape(1, n_out))


def _rope(y, cos_t, sin_t):
    blocks = []
    for p in range(y.shape[1] // LANES):
        yp = y[:, p * LANES:(p + 1) * LANES]
        blocks.append(yp * cos_t + pltpu.roll(yp, LANES // 2, 1) * sin_t)
    return blocks[0] if len(blocks) == 1 else jnp.concatenate(blocks, axis=1)


def _proj_kernel(*refs, latent):
    if latent:
        (x_ref, mod_ref, g_ref, w_ref, gq_ref, gk_ref, seg_ref, cos_ref, sin_ref,
         qa_ref, ka_ref, va_ref, qr_ref, kr_ref, vr_ref, gr_ref) = refs
    else:
        (x_ref, mod_ref, g_ref, w_ref, gk_ref,
         ka_ref, va_ref, kr_ref, vr_ref) = refs

    x = x_ref[0]
    ms = jnp.mean(x * x, axis=-1, keepdims=True)
    shift = mod_ref[0, 0:1, :]
    scale = mod_ref[0, 1:2, :]
    h = (x * lax.rsqrt(ms + EPS) * g_ref[...]) * (1.0 + scale) + shift
    hb = h.astype(BF16)

    def proj(name):
        start, width = _PROJ_COLS[name]
        return _dot(hb, w_ref[:, start:start + width])

    if latent:
        cos_t = cos_ref[...]
        sin_t = sin_ref[...]
        rope = lambda y: _rope(y, cos_t, sin_t)
        yq = proj("qa")
        ssq = _split_dot(yq * yq, seg_ref[...])
        gq = jnp.concatenate([gq_ref[...]] * (ATTN_WIDTH // LANES), axis=1)
        qn = yq * lax.rsqrt(ssq * (1.0 / HEAD_DIM) + EPS) * gq
        qa_ref[0] = (rope(qn) * (ATTN_SCALE * LOG2E)).T.astype(BF16)
    else:
        rope = lambda y: y

    yk = proj("ka")
    slot_b = (_lane_iota((yk.shape[0], LANES)) // HALF) % 2 == 1
    kblocks = []
    for kv in range(ATTN_KV_HEADS):
        ykv = yk[:, kv * LANES:(kv + 1) * LANES]
        ssq = 0.5 * jnp.sum(ykv * ykv, axis=-1, keepdims=True)
        kn = rope(ykv * lax.rsqrt(ssq * (1.0 / HEAD_DIM) + EPS) * gk_ref[...])
        kblocks += [jnp.where(slot_b, 0.0, kn), jnp.where(slot_b, kn, 0.0)]
    ka_ref[0] = jnp.concatenate(kblocks, axis=1).astype(BF16)

    yv = proj("va")
    ones_lane = (_lane_iota(yv.shape) % LANES) >= HEAD_DIM
    va_ref[0, 0] = jnp.where(ones_lane, 1.0, yv).T.astype(BF16)

    if latent:
        qr_ref[0] = rope(proj("qr")).astype(BF16)
    kr_ref[0] = rope(proj("kr") * ATTN_SCALE).astype(BF16)
    vr_ref[0] = proj("vr").astype(BF16)
    if latent:
        gr_ref[0] = proj("gr")


def _proj_call(x, mod, g_pre, w, gq_tab, gk_tab, seg, cos_t, sin_t, *, latent, tm):
    bsz, n, _ = x.shape
    wcols = w.shape[1]
    tok = lambda width: pl.BlockSpec((1, tm, width), lambda b, i: (b, i, 0))
    full = lambda shape: pl.BlockSpec(shape, lambda b, i: (0,) * len(shape))
    mod_spec = pl.BlockSpec((1, N_MOD, D_MODEL), (lambda b, i: (b, 0, 0)) if latent
                            else (lambda b, i: (0, 0, 0)))
    if latent:
        in_specs = [tok(D_MODEL), mod_spec, full((1, D_MODEL)), full((D_MODEL, wcols)),
                    full((1, LANES)), full((1, LANES)), full((ATTN_WIDTH, ATTN_WIDTH)),
                    pl.BlockSpec((tm, LANES), lambda b, i: (i, 0)),
                    pl.BlockSpec((tm, LANES), lambda b, i: (i, 0))]
        args = (x, mod, g_pre, w, gq_tab, gk_tab, seg, cos_t, sin_t)
    else:
        in_specs = [tok(D_MODEL), mod_spec, full((1, D_MODEL)), full((D_MODEL, wcols)),
                    full((1, LANES))]
        args = (x, mod, g_pre, w, gk_tab)
    token_major = lambda width, dt: (jax.ShapeDtypeStruct((bsz, n, width), dt), tok(width))
    q_t = (jax.ShapeDtypeStruct((bsz, ATTN_WIDTH, n), BF16),
           pl.BlockSpec((1, ATTN_WIDTH, tm), lambda b, i: (b, 0, i)))
    k_ab = token_major(4 * KV_WIDTH, BF16)
    v_t = (jax.ShapeDtypeStruct((bsz, n // tm, 2 * KV_WIDTH, tm), BF16),
           pl.BlockSpec((1, 1, 2 * KV_WIDTH, tm), lambda b, i: (b, i, 0, 0)))
    ret = token_major(RET_WIDTH, BF16)
    outs = ([q_t, k_ab, v_t, ret, ret, ret, token_major(RET_WIDTH, F32)] if latent
            else [k_ab, v_t, ret, ret])
    return pl.pallas_call(
        functools.partial(_proj_kernel, latent=latent),
        out_shape=[o[0] for o in outs],
        grid=(bsz, n // tm),
        in_specs=in_specs,
        out_specs=[o[1] for o in outs],
        compiler_params=pltpu.CompilerParams(dimension_semantics=("arbitrary", "arbitrary"),
                                             vmem_limit_bytes=VMEM_LIMIT),
        name="in_proj_latent" if latent else "in_proj_ctx",
    )(*args)


LOOKAHEAD = 3
TILES_PER_TRIP = 4


def _attn_kernel(qt_ref, kc_ref, vc_ref, kl_ref, vl_ref, o_ref, m_ref, acc_ref, s_ref):
    n_heads = 2 * (qt_ref.shape[1] // LANES)

    m_ref[...] = jnp.full(m_ref.shape, -jnp.inf, F32)
    acc_ref[...] = jnp.zeros(acc_ref.shape, F32)

    n_tiles = kl_ref.shape[1]

    def scores(k_ref, h):
        q_t = qt_ref[0, (h // 2) * LANES:(h // 2 + 1) * LANES, :]
        half = k_ref.shape[0] // 2
        cols = slice((h % 2) * LANES, (h % 2 + 1) * LANES)
        return (_dot(k_ref[:half, cols], q_t), _dot(k_ref[half:, cols], q_t))

    def accumulate(h, s, v_ref):
        half = s[0].shape[0]
        m_old = m_ref[h]
        m_new = jnp.maximum(m_old, jnp.maximum(jnp.max(s[0], axis=0, keepdims=True),
                                               jnp.max(s[1], axis=0, keepdims=True)))
        alpha = jnp.exp2(m_old - m_new)
        p0 = jnp.exp2(s[0] - m_new).astype(BF16)
        p1 = jnp.exp2(s[1] - m_new).astype(BF16)
        acc_ref[h] = alpha * acc_ref[h] + (_dot(v_ref[:, :half], p0) + _dot(v_ref[:, half:], p1))
        m_ref[h] = m_new

    def step(s_ahead, k_ref, v_ref, k_after_ref):
        s_queue = list(s_ahead)
        for h in range(n_heads):
            nxt = h + LOOKAHEAD
            s_queue.append(scores(k_ref, nxt) if nxt < n_heads
                           else scores(k_after_ref, nxt - n_heads))
            accumulate(h, s_queue.pop(0), v_ref)
        return s_queue

    def save(s_ahead):
        for i, s in enumerate(s_ahead):
            s_ref[i, 0] = s[0]
            s_ref[i, 1] = s[1]

    ctx_k = kc_ref.at[0]
    save(step([scores(ctx_k, i) for i in range(LOOKAHEAD)], ctx_k, vc_ref.at[0, 0], kl_ref.at[0, 0]))

    def body(jj, carry):
        s_ahead = [(s_ref[i, 0], s_ref[i, 1]) for i in range(LOOKAHEAD)]
        for u in range(TILES_PER_TRIP):
            j = jj * TILES_PER_TRIP + u
            s_ahead = step(s_ahead, kl_ref.at[0, j], vl_ref.at[0, j],
                           kl_ref.at[0, jnp.minimum(j + 1, n_tiles - 1)])
        save(s_ahead)
        return carry

    assert n_tiles % TILES_PER_TRIP == 0
    lax.fori_loop(0, n_tiles // TILES_PER_TRIP, body, 0)

    heads = []
    for h in range(n_heads):
        acc = acc_ref[h]
        heads.append(acc[:HEAD_DIM] / acc[HEAD_DIM:])
    o_ref[0] = jnp.concatenate(heads, axis=0).T.astype(o_ref.dtype)


def _attn_call(qa_t, ka_c, va_c, ka_l, va_l, *, tq):
    bsz, _, n = qa_t.shape
    n_ctx = ka_c.shape[1]
    n_tiles, tk = ka_l.shape[1], ka_l.shape[2]
    gw = ATTN_WIDTH // ATTN_KV_HEADS
    n_heads = ATTN_HEADS // ATTN_KV_HEADS
    return pl.pallas_call(
        _attn_kernel,
        out_shape=jax.ShapeDtypeStruct((bsz, n, ATTN_WIDTH), BF16),
        grid=(bsz, ATTN_KV_HEADS, n // tq),
        in_specs=[pl.BlockSpec((1, gw, tq), lambda b, g, i: (b, g, i)),
                  pl.BlockSpec((1, n_ctx, 2 * LANES), lambda b, g, i: (b, 0, g)),
                  pl.BlockSpec((1, 1, LANES, n_ctx), lambda b, g, i: (b, 0, g, 0)),
                  pl.BlockSpec((1, n_tiles, tk, 2 * LANES), lambda b, g, i: (b, 0, 0, g)),
                  pl.BlockSpec((1, n_tiles, LANES, tk), lambda b, g, i: (b, 0, g, 0))],
        out_specs=pl.BlockSpec((1, tq, gw), lambda b, g, i: (b, i, g)),
        scratch_shapes=[pltpu.VMEM((n_heads, 1, tq), F32),
                        pltpu.VMEM((n_heads, LANES, tq), F32),
                        pltpu.VMEM((LOOKAHEAD, 2, tk // 2, tq), F32)],
        compiler_params=pltpu.CompilerParams(
            dimension_semantics=("arbitrary", "arbitrary", "arbitrary"),
            vmem_limit_bytes=VMEM_LIMIT),
        name="gqa_flash_attention",
    )(qa_t, ka_c, va_c, ka_l, va_l)


N_PAIRS = RET_HEADS // 2


def _ret_kernel(lg_head_ref, lg_std_ref, lg_pair_ref,
                qm_ref, km_ref, vm_ref, qb_ref, kb_ref, vb_ref, kc_ref, vc_ref,
                om_ref, ob_ref,
                dsum_ref, qdec_ref, kdec_ref, cdec_ref, state_ref):
    C = RET_CHUNK
    n_ctx = kc_ref.shape[1]
    first_call = (pl.program_id(0) == 0) & (pl.program_id(1) == 0)

    std_a = _lane_iota((C, LANES)) < HEAD_DIM
    pair_a = (_lane_iota((C, LANES)) // HALF) % 2 == 0
    row_head = (lax.broadcasted_iota(jnp.int32, (LANES, LANES), 0) // HALF) % 2
    col_head = lax.broadcasted_iota(jnp.int32, (LANES, LANES), 1) // HEAD_DIM
    same_head = row_head == col_head

    @pl.when(first_call)
    def _tables():
        row = lax.broadcasted_iota(jnp.int32, (C, C), 0).astype(F32)
        colv = lax.broadcasted_iota(jnp.int32, (C, C), 1).astype(F32)
        pos = row[:, :LANES]
        for h in range(RET_HEADS):
            lg_f = -jnp.exp(lg_head_ref[0, h])
            lg_b = -jnp.exp(lg_head_ref[1, h])
            dsum_ref[h] = (jnp.where(row >= colv, jnp.exp(lg_f * jnp.maximum(row - colv, 0.0)), 0.0)
                           + jnp.where(colv >= row, jnp.exp(lg_b * jnp.maximum(colv - row, 0.0)), 0.0))
        for d in range(2):
            for p in range(N_PAIRS):
                lg_std = -jnp.exp(lg_std_ref[d, p])
                lg_pair = -jnp.exp(lg_pair_ref[d, p])
                q_steps = (pos + 1.0) if d == 0 else (C - pos)
                k_steps = (C - 1.0 - pos) if d == 0 else pos
                qdec_ref[d, p] = jnp.exp(lg_std * q_steps)
                kdec_ref[d, p] = jnp.exp(lg_pair * k_steps)
                cdec_ref[d, p] = jnp.exp(lg_std * float(C))

    @pl.when(pl.program_id(1) == 0)
    def _seed_states():
        cpos = lax.broadcasted_iota(jnp.int32, (n_ctx, LANES), 0).astype(F32)
        for d in range(2):
            steps = (n_ctx - 1.0 - cpos) if d == 0 else cpos
            for p in range(N_PAIRS):
                lg_pair = -jnp.exp(lg_pair_ref[d, p])
                kc = kc_ref[0, :, p * LANES:(p + 1) * LANES].astype(F32) * jnp.exp(lg_pair * steps)
                vc = vc_ref[0, :, p * LANES:(p + 1) * LANES]
                state_ref[d, p] = jnp.where(same_head, _dot_tn(kc.astype(BF16), vc), 0.0)

    def decayed(x, dec):
        return (x.astype(F32) * dec).astype(BF16)

    stage1 = []
    for p in range(N_PAIRS):
        sl = slice(p * LANES, (p + 1) * LANES)
        q, k, v = qm_ref[0, :, sl], km_ref[0, :, sl], vm_ref[0, :, sl]
        q2, k2, v2 = qb_ref[0, :, sl], kb_ref[0, :, sl], vb_ref[0, :, sl]
        zk = jnp.zeros_like(k)
        k_split = jnp.concatenate([jnp.where(pair_a, k, zk), jnp.where(pair_a, zk, k)], axis=0)
        s = _dot_nt(q, k_split)
        cross_f = _dot(q, state_ref[0, p].astype(BF16)) * qdec_ref[0, p]
        cross_b = _dot(q2, state_ref[1, p].astype(BF16)) * qdec_ref[1, p]
        kv_f = _dot_tn(decayed(k, kdec_ref[0, p]), v)
        kv_b = _dot_tn(decayed(k2, kdec_ref[1, p]), v2)
        stage1.append((s, cross_f, cross_b, kv_f, kv_b, v))

    for p, (s, cross_f, cross_b, kv_f, kv_b, v) in enumerate(stage1):
        sl = slice(p * LANES, (p + 1) * LANES)
        w = jnp.concatenate([s[:, :C] * dsum_ref[2 * p], s[:, C:] * dsum_ref[2 * p + 1]], axis=1)
        zv = jnp.zeros_like(v)
        v_split = jnp.concatenate([jnp.where(std_a, v, zv), jnp.where(std_a, zv, v)], axis=0)
        om_ref[0, :, sl] = _dot(w.astype(BF16), v_split) + cross_f
        ob_ref[0, :, sl] = cross_b
        state_ref[0, p] = state_ref[0, p] * cdec_ref[0, p] + jnp.where(same_head, kv_f, 0.0)
        state_ref[1, p] = state_ref[1, p] * cdec_ref[1, p] + jnp.where(same_head, kv_b, 0.0)


def _ret_call(lg_head, lg_std, lg_pair, qr, kr, vr, kr_c, vr_c):
    bsz, n, _ = qr.shape
    n_ctx = kr_c.shape[1]
    C = RET_CHUNK
    nc = n // C
    fwd = pl.BlockSpec((1, C, RET_WIDTH), lambda b, c: (b, c, 0))
    bwd = pl.BlockSpec((1, C, RET_WIDTH), lambda b, c: (b, nc - 1 - c, 0))
    ctx = pl.BlockSpec((1, n_ctx, RET_WIDTH), lambda b, c: (b, 0, 0))
    full = lambda a: pl.BlockSpec(a.shape, lambda b, c: (0,) * a.ndim)
    return pl.pallas_call(
        _ret_kernel,
        out_shape=[jax.ShapeDtypeStruct((bsz, n, RET_WIDTH), F32)] * 2,
        grid=(bsz, nc),
        in_specs=[full(lg_head), full(lg_std), full(lg_pair), fwd, fwd, fwd, bwd, bwd, bwd, ctx, ctx],
        out_specs=[fwd, bwd],
        scratch_shapes=[pltpu.VMEM((RET_HEADS, C, C), F32),
                        pltpu.VMEM((2, N_PAIRS, C, LANES), F32),
                        pltpu.VMEM((2, N_PAIRS, C, LANES), F32),
                        pltpu.VMEM((2, N_PAIRS, 1, LANES), F32),
                        pltpu.VMEM((2, N_PAIRS, LANES, LANES), F32)],
        compiler_params=pltpu.CompilerParams(dimension_semantics=("arbitrary", "arbitrary"),
                                             vmem_limit_bytes=VMEM_LIMIT),
        name="bidir_retention",
    )(lg_head, lg_std, lg_pair, qr, kr, vr, qr, kr, vr, kr_c, vr_c)


def _merge_ffn_kernel(x_ref, att_ref, of_ref, ob_ref, gr_ref, mod_ref, gpm_ref, gpf_ref, gqf_ref,
                      seg_ref, wo_ref, wi_ref, w2_ref, out_ref):
    x = x_ref[0]
    gate_m = mod_ref[0, 2:3, :]
    shift_f = mod_ref[0, 3:4, :]
    scale_f = mod_ref[0, 4:5, :]
    gate_f = mod_ref[0, 5:6, :]

    ret = of_ref[0] + ob_ref[0]
    seg = seg_ref[...]
    mu = _split_dot(ret, seg) * (1.0 / HEAD_DIM)
    cen = ret - mu
    var = _split_dot(cen * cen, seg) * (1.0 / HEAD_DIM)
    ret_n = cen * lax.rsqrt(var + EPS) * _silu(gr_ref[0])

    mix = (_dot(att_ref[0], wo_ref[:ATTN_WIDTH, :])
           + _dot(ret_n.astype(BF16), wo_ref[ATTN_WIDTH:, :]))
    ms = jnp.mean(mix * mix, axis=-1, keepdims=True)
    x1 = x + gate_m * (mix * lax.rsqrt(ms + EPS) * gpm_ref[...])

    ms1 = jnp.mean(x1 * x1, axis=-1, keepdims=True)
    hf = ((x1 * lax.rsqrt(ms1 + EPS) * gpf_ref[...]) * (1.0 + scale_f) + shift_f).astype(BF16)

    acc = jnp.zeros(x.shape, F32)
    for j in range(FFN_HIDDEN // FFN_CHUNK):
        lo = j * FFN_CHUNK
        a = _dot(hf, wi_ref[:, lo:lo + FFN_CHUNK])
        b = _dot(hf, wi_ref[:, FFN_HIDDEN + lo:FFN_HIDDEN + lo + FFN_CHUNK])
        acc = acc + _dot((_silu(a) * b).astype(BF16), w2_ref[lo:lo + FFN_CHUNK, :])
    ms2 = jnp.mean(acc * acc, axis=-1, keepdims=True)
    out_ref[0] = x1 + gate_f * (acc * lax.rsqrt(ms2 + EPS) * gqf_ref[...])


def _merge_ffn_call(x, att, o_f, o_b, gr, mod, g_post_mix, g_pre_ffn, g_post_ffn, seg, w_out,
                    w_ffn_in, w_ffn_out, *, tm):
    bsz, n, _ = x.shape
    tok = lambda width: pl.BlockSpec((1, tm, width), lambda b, i: (b, i, 0))
    full = lambda a: pl.BlockSpec(a.shape, lambda b, i: (0,) * a.ndim,
                                  pipeline_mode=pl.Buffered(1))
    return pl.pallas_call(
        _merge_ffn_kernel,
        out_shape=jax.ShapeDtypeStruct(x.shape, F32),
        grid=(bsz, n // tm),
        in_specs=[tok(D_MODEL), tok(ATTN_WIDTH), tok(RET_WIDTH), tok(RET_WIDTH), tok(RET_WIDTH),
                  pl.BlockSpec((1, N_MOD, D_MODEL), lambda b, i: (b, 0, 0)),
                  full(g_post_mix), full(g_pre_ffn), full(g_post_ffn), full(seg),
                  full(w_out), full(w_ffn_in), full(w_ffn_out)],
        out_specs=tok(D_MODEL),
        compiler_params=pltpu.CompilerParams(dimension_semantics=("arbitrary", "arbitrary"),
                                             vmem_limit_bytes=VMEM_LIMIT),
        name="merge_outproj_ffn",
    )(x, att, o_f, o_b, gr, mod, g_post_mix, g_pre_ffn, g_post_ffn, seg, w_out, w_ffn_in, w_ffn_out)


def _rope_tables(n_lat):
    tok = jnp.arange(n_lat)
    row = (tok // GRID_W).astype(F32)[:, None]
    col = (tok % GRID_W).astype(F32)[:, None]
    n_freq = HEAD_DIM // 4
    inv = ROPE_BASE ** (-jnp.arange(n_freq, dtype=F32) / n_freq)
    zero = jnp.zeros_like(inv)
    row_inv = jnp.tile(jnp.concatenate([inv, zero]), LANES // HALF)[None, :]
    col_inv = jnp.tile(jnp.concatenate([zero, inv]), LANES // HALF)[None, :]
    ang = row * row_inv + col * col_inv
    sign = jnp.asarray(np.where(np.arange(LANES) < LANES // 2, -1.0, 1.0), F32)[None, :]
    return jnp.cos(ang), jnp.sin(ang) * sign


def _segment_matrix(width, same):
    i = np.arange(width)
    return jnp.asarray(same(i[:, None], i[None, :]), BF16)


def kernel(x, c, ctx, c_ctx, w_mod, b_mod, g_pre_mix, g_post_mix, g_pre_ffn, g_post_ffn,
           w_in, q_norm_g, k_norm_g, ret_decay_fwd, ret_decay_bwd, w_out, w_ffn_in, w_ffn_out):
    bsz, n_lat, _ = x.shape
    assert w_mod.shape[0] == 1, "single-layer block"
    layer = 0

    w_proj = jnp.take(w_in[layer], _PROJ_GATHER, axis=1, mode="fill", fill_value=0).astype(BF16)
    gq_tab = q_norm_g[layer][_PAIR_DIM].reshape(1, LANES)
    gk_tab = k_norm_g[layer][_PAIR_DIM].reshape(1, LANES)
    seg_pair = _segment_matrix(ATTN_WIDTH, lambda i, j: (i // LANES == j // LANES)
                               & ((i // HALF) % 2 == (j // HALF) % 2))
    seg_std = _segment_matrix(RET_WIDTH, lambda i, j: i // HEAD_DIM == j // HEAD_DIM)
    cos_t, sin_t = _rope_tables(n_lat)

    decay = jnp.stack([ret_decay_fwd[layer], ret_decay_bwd[layer]]).astype(F32)
    lg_head = jnp.broadcast_to(decay[:, :, None, None], (2, RET_HEADS, 1, LANES))
    std_head = np.arange(RET_WIDTH) // HEAD_DIM
    lg_std = decay[:, std_head].reshape(2, N_PAIRS, 1, LANES)
    lg_pair = decay[:, _PAIR8 // HEAD_DIM].reshape(2, N_PAIRS, 1, LANES)

    w_ffn_in_b = w_ffn_in[layer].astype(BF16)
    w_ffn_out_b = w_ffn_out[layer].astype(BF16)
    w_out_b = w_out[layer].astype(BF16)

    c_rows = jnp.concatenate([c, c_ctx[None, :], jnp.zeros((8 - bsz - 1, D_MODEL), c.dtype)], axis=0)
    mod = _mod_call(c_rows, w_mod[layer], b_mod[layer]).reshape(8, N_MOD, D_MODEL)

    g_pre = g_pre_mix[layer].reshape(1, D_MODEL)
    qa, ka_l, va_l, qr, kr, vr, gr = _proj_call(x, mod, g_pre, w_proj, gq_tab, gk_tab, seg_pair,
                                                cos_t, sin_t, latent=True, tm=512)
    ka_c, va_c, kr_c, vr_c = _proj_call(ctx, mod[bsz:bsz + 1], g_pre, w_proj, None, gk_tab, None,
                                        None, None, latent=False, tm=ctx.shape[1])

    tk = va_l.shape[-1]
    att = _attn_call(qa, ka_c, va_c, ka_l.reshape(bsz, n_lat // tk, tk, 4 * KV_WIDTH), va_l, tq=256)
    o_f, o_b = _ret_call(lg_head, lg_std, lg_pair, qr, kr, vr, kr_c, vr_c)

    return _merge_ffn_call(x, att, o_f, o_b, gr, mod,
                           g_post_mix[layer].reshape(1, D_MODEL),
                           g_pre_ffn[layer].reshape(1, D_MODEL),
                           g_post_ffn[layer].reshape(1, D_MODEL),
                           seg_std, w_out_b, w_ffn_in_b, w_ffn_out_b, tm=512)
```

```python
import functools
import math

import numpy as np
import jax
import jax.numpy as jnp
from jax import lax
from jax.experimental import pallas as pl
from jax.experimental.pallas import tpu as pltpu

D_MODEL = 1024
HEAD_DIM = 64
HALF = HEAD_DIM // 2
GRID_W = 64
ATTN_HEADS = 8
ATTN_KV_HEADS = 2
RET_HEADS = 8
ATTN_WIDTH = ATTN_HEADS * HEAD_DIM
KV_WIDTH = ATTN_KV_HEADS * HEAD_DIM
RET_WIDTH = RET_HEADS * HEAD_DIM
FFN_HIDDEN = 2816
FFN_CHUNK = 256
N_MOD = 6
RET_CHUNK = 128
RET_CHUNKS_PER_STEP = 2
ROPE_BASE = 10000.0
ATTN_SCALE = HEAD_DIM ** -0.5
EPS = 1e-6
LOG2E = math.log2(math.e)

LANES = 128
MXU_DEPTH = 256
VMEM_LIMIT = 56 * 1024 * 1024

F32 = jnp.float32
BF16 = jnp.bfloat16

_QA0 = 0
_KA0 = ATTN_WIDTH
_VA0 = ATTN_WIDTH + KV_WIDTH
_QR0 = ATTN_WIDTH + 2 * KV_WIDTH
_KR0 = _QR0 + RET_WIDTH
_VR0 = _KR0 + RET_WIDTH
_GR0 = _VR0 + RET_WIDTH


def _pair_perm(n_heads):
    idx = []
    for p in range(n_heads // 2):
        for seg in range(4):
            head = 2 * p + (seg % 2)
            lo = (seg // 2) * HALF
            idx.extend(head * HEAD_DIM + lo + d for d in range(HALF))
    return np.asarray(idx, np.int32)


def _dup_perm(n_heads):
    idx = []
    for h in range(n_heads):
        for seg in range(4):
            lo = (seg // 2) * HALF
            idx.extend(h * HEAD_DIM + lo + d for d in range(HALF))
    return np.asarray(idx, np.int32)


_PAIR8 = _pair_perm(8)
_DUP2 = _dup_perm(ATTN_KV_HEADS)


def _projection_layout():
    std = np.arange(RET_WIDTH, dtype=np.int32)
    zeros = np.full(HEAD_DIM, _GR0 + RET_WIDTH, np.int32)
    v0 = _VA0 + np.arange(HEAD_DIM, dtype=np.int32)
    groups = [("qa", _QA0 + _PAIR8), ("ka", _KA0 + _DUP2),
              ("va", np.concatenate([v0, zeros, v0 + HEAD_DIM, zeros])),
              ("qr", _QR0 + _PAIR8), ("kr", _KR0 + _PAIR8), ("vr", _VR0 + std), ("gr", _GR0 + std)]
    cols, start = {}, 0
    for name, idx in groups:
        cols[name] = (start, len(idx))
        start += len(idx)
    return np.concatenate([idx for _, idx in groups]), cols


_PROJ_GATHER, _PROJ_COLS = _projection_layout()
_PAIR_DIM = (_PAIR8[:LANES] % HEAD_DIM).astype(np.int32)


def _dot(a, b):
    return jnp.dot(a, b, preferred_element_type=F32)


def _dot_nt(a, b):
    return lax.dot_general(a, b, (((1,), (1,)), ((), ())), preferred_element_type=F32)


def _dot_tn(a, b):
    return lax.dot_general(a, b, (((0,), (0,)), ((), ())), preferred_element_type=F32)


def _head_sums(a, same_head_bf16, exact=False):
    hi = a.astype(BF16)
    out = _dot(hi, same_head_bf16)
    if exact:
        out = out + _dot((a - hi.astype(F32)).astype(BF16), same_head_bf16)
    return out


def _silu(x):
    return x * (1.0 / (1.0 + jnp.exp(-x)))


def _lane_iota(shape):
    return lax.broadcasted_iota(jnp.int32, shape, len(shape) - 1)


def _mod_kernel(c_ref, w_ref, b_ref, o_ref):
    h = _silu(c_ref[...]).astype(BF16)
    o_ref[...] = _dot(h, w_ref[...].astype(BF16)) + b_ref[...]


def _mod_call(c_rows, w_mod, b_mod):
    rows = c_rows.shape[0]
    n_out = w_mod.shape[1]
    tn = D_MODEL
    return pl.pallas_call(
        _mod_kernel,
        out_shape=jax.ShapeDtypeStruct((rows, n_out), F32),
        grid=(n_out // tn,),
        in_specs=[
            pl.BlockSpec((rows, D_MODEL), lambda j: (0, 0)),
            pl.BlockSpec((D_MODEL, tn), lambda j: (0, j)),
            pl.BlockSpec((1, tn), lambda j: (0, j)),
        ],
        out_specs=pl.BlockSpec((rows, tn), lambda j: (0, j)),
        compiler_params=pltpu.CompilerParams(dimension_semantics=("arbitrary",),
                                             vmem_limit_bytes=VMEM_LIMIT),
        name="adaln_mod",
    )(c_rows, w_mod, b_mod.reshape(1, n_out))


def _rope(y, cos_t, sin_t):
    blocks = []
    for p in range(y.shape[1] // LANES):
        yp = y[:, p * LANES:(p + 1) * LANES]
        blocks.append(yp * cos_t + pltpu.roll(yp, LANES // 2, 1) * sin_t)
    return blocks[0] if len(blocks) == 1 else jnp.concatenate(blocks, axis=1)


def _proj_kernel(*refs, latent):
    if latent:
        (x_ref, mod_ref, g_ref, w_ref, gq_ref, gk_ref, seg_ref, cos_ref, sin_ref,
         qa_ref, ka_ref, va_ref, qr_ref, kr_ref, vr_ref, gr_ref) = refs
    else:
        (x_ref, mod_ref, g_ref, w_ref, gk_ref,
         ka_ref, va_ref, kr_ref, vr_ref) = refs

    x = x_ref[0]
    ms = jnp.mean(x * x, axis=-1, keepdims=True)
    shift = mod_ref[0, 0:1, :]
    scale = mod_ref[0, 1:2, :]
    h = (x * lax.rsqrt(ms + EPS) * g_ref[...]) * (1.0 + scale) + shift
    hb = h.astype(BF16)

    def proj(name):
        start, width = _PROJ_COLS[name]
        return _dot(hb, w_ref[:, start:start + width])

    if latent:
        cos_t = cos_ref[...]
        sin_t = sin_ref[...]
        rope = lambda y: _rope(y, cos_t, sin_t)
        yq = proj("qa")
        ssq = _head_sums(yq * yq, seg_ref[...])
        gq = jnp.concatenate([gq_ref[...]] * (ATTN_WIDTH // LANES), axis=1)
        qn = yq * lax.rsqrt(ssq * (1.0 / HEAD_DIM) + EPS) * gq
        qa_ref[0] = (rope(qn) * (ATTN_SCALE * LOG2E)).T.astype(BF16)
    else:
        rope = lambda y: y

    yk = proj("ka")
    slot_b = (_lane_iota((yk.shape[0], LANES)) // HALF) % 2 == 1
    kblocks = []
    for kv in range(ATTN_KV_HEADS):
        ykv = yk[:, kv * LANES:(kv + 1) * LANES]
        ssq = 0.5 * jnp.sum(ykv * ykv, axis=-1, keepdims=True)
        kn = rope(ykv * lax.rsqrt(ssq * (1.0 / HEAD_DIM) + EPS) * gk_ref[...])
        kblocks += [jnp.where(slot_b, 0.0, kn), jnp.where(slot_b, kn, 0.0)]
    ka_ref[0] = jnp.concatenate(kblocks, axis=1).astype(BF16)

    yv = proj("va")
    ones_lane = (_lane_iota(yv.shape) % LANES) >= HEAD_DIM
    va_ref[0, 0] = jnp.where(ones_lane, 1.0, yv).T.astype(BF16)

    if latent:
        qr_ref[0] = rope(proj("qr")).astype(BF16)
    kr_ref[0] = rope(proj("kr") * ATTN_SCALE).astype(BF16)
    vr_ref[0] = proj("vr").astype(BF16)
    if latent:
        gr_ref[0] = proj("gr")


def _proj_call(x, mod, g_pre, w, gq_tab, gk_tab, seg, cos_t, sin_t, *, latent, tm):
    bsz, n, _ = x.shape
    wcols = w.shape[1]
    tok = lambda width: pl.BlockSpec((1, tm, width), lambda b, i: (b, i, 0))
    full = lambda shape: pl.BlockSpec(shape, lambda b, i: (0,) * len(shape))
    mod_spec = pl.BlockSpec((1, N_MOD, D_MODEL), (lambda b, i: (b, 0, 0)) if latent
                            else (lambda b, i: (0, 0, 0)))
    if latent:
        in_specs = [tok(D_MODEL), mod_spec, full((1, D_MODEL)), full((D_MODEL, wcols)),
                    full((1, LANES)), full((1, LANES)), full((ATTN_WIDTH, ATTN_WIDTH)),
                    pl.BlockSpec((tm, LANES), lambda b, i: (i, 0)),
                    pl.BlockSpec((tm, LANES), lambda b, i: (i, 0))]
        args = (x, mod, g_pre, w, gq_tab, gk_tab, seg, cos_t, sin_t)
    else:
        in_specs = [tok(D_MODEL), mod_spec, full((1, D_MODEL)), full((D_MODEL, wcols)),
                    full((1, LANES))]
        args = (x, mod, g_pre, w, gk_tab)
    token_major = lambda width, dt: (jax.ShapeDtypeStruct((bsz, n, width), dt), tok(width))
    q_t = (jax.ShapeDtypeStruct((bsz, ATTN_WIDTH, n), BF16),
           pl.BlockSpec((1, ATTN_WIDTH, tm), lambda b, i: (b, 0, i)))
    k_ab = token_major(4 * KV_WIDTH, BF16)
    v_t = (jax.ShapeDtypeStruct((bsz, n // tm, 2 * KV_WIDTH, tm), BF16),
           pl.BlockSpec((1, 1, 2 * KV_WIDTH, tm), lambda b, i: (b, i, 0, 0)))
    ret = token_major(RET_WIDTH, BF16)
    outs = ([q_t, k_ab, v_t, ret, ret, ret, token_major(RET_WIDTH, F32)] if latent
            else [k_ab, v_t, ret, ret])
    return pl.pallas_call(
        functools.partial(_proj_kernel, latent=latent),
        out_shape=[o[0] for o in outs],
        grid=(bsz, n // tm),
        in_specs=in_specs,
        out_specs=[o[1] for o in outs],
        compiler_params=pltpu.CompilerParams(dimension_semantics=("arbitrary", "arbitrary"),
                                             vmem_limit_bytes=VMEM_LIMIT),
        name="in_proj_latent" if latent else "in_proj_ctx",
    )(*args)


LOOKAHEAD = 3
TILES_PER_TRIP = 8
ITEM_KEYS = 512
DRAIN_ITEMS = 2


def _attn_kernel(qt_ref, kc_ref, vc_ref, kl_ref, vl_ref, o_ref, m_ref, acc_ref, s_ref):
    n_heads = 2 * (qt_ref.shape[1] // LANES)

    m_ref[...] = jnp.full(m_ref.shape, -jnp.inf, F32)
    acc_ref[...] = jnp.zeros(acc_ref.shape, F32)

    n_tiles = kl_ref.shape[1]

    def scores(k_ref, item):
        sub, h = divmod(item, n_heads)
        q_t = qt_ref[0, (h // 2) * LANES:(h // 2 + 1) * LANES, :]
        keys = min(ITEM_KEYS, k_ref.shape[0])
        half = keys // 2
        lo = sub * keys
        cols = slice((h % 2) * LANES, (h % 2 + 1) * LANES)
        return (_dot(k_ref[lo:lo + half, cols], q_t), _dot(k_ref[lo + half:lo + keys, cols], q_t))

    def accumulate(item, s, v_ref):
        sub, h = divmod(item, n_heads)
        half = s[0].shape[0]
        lo = sub * 2 * half
        m_old = m_ref[h]
        m_new = jnp.maximum(m_old, jnp.maximum(jnp.max(s[0], axis=0, keepdims=True),
                                               jnp.max(s[1], axis=0, keepdims=True)))
        alpha = jnp.exp2(m_old - m_new)
        p0 = jnp.exp2(s[0] - m_new).astype(BF16)
        p1 = jnp.exp2(s[1] - m_new).astype(BF16)
        if 2 * half <= MXU_DEPTH:
            pv = _dot(v_ref[:, lo:lo + 2 * half], jnp.concatenate([p0, p1], axis=0))
        else:
            pv = _dot(v_ref[:, lo:lo + half], p0) + _dot(v_ref[:, lo + half:lo + 2 * half], p1)
        acc_ref[h] = alpha * acc_ref[h] + pv
        m_ref[h] = m_new

    def run_items(s_ahead, tiles, after):
        per_tile = [n_heads * max(1, k_ref.shape[0] // ITEM_KEYS) for k_ref, _ in tiles]
        refs = [(k_ref, v_ref, i) for (k_ref, v_ref), n in zip(tiles, per_tile) for i in range(n)]
        refs_after = [(after, None, i) for i in range(LOOKAHEAD)]
        total = len(refs)
        requests = [[i + LOOKAHEAD] for i in range(total)]
        drain = min(DRAIN_ITEMS, total // 2)
        for i in range(total - drain, total):
            requests[i - drain] += requests[i]
            requests[i] = []
        pending = dict(enumerate(s_ahead))
        for i, (k_ref, v_ref, item) in enumerate(refs):
            for r in requests[i]:
                rk, _, ritem = (refs + refs_after)[r]
                pending[r] = scores(rk, ritem)
            accumulate(item, pending.pop(i), v_ref)
        return [pending[total + j] for j in range(LOOKAHEAD)]

    def save(s_ahead):
        for i, s in enumerate(s_ahead):
            s_ref[i, 0] = s[0]
            s_ref[i, 1] = s[1]

    ctx_k = kc_ref.at[0]
    save(run_items([scores(ctx_k, i) for i in range(LOOKAHEAD)], [(ctx_k, vc_ref.at[0, 0])],
                   kl_ref.at[0, 0]))

    def body(jj, carry):
        j0 = jj * TILES_PER_TRIP
        tiles = [(kl_ref.at[0, j0 + u], vl_ref.at[0, j0 + u]) for u in range(TILES_PER_TRIP)]
        after = kl_ref.at[0, jnp.minimum(j0 + TILES_PER_TRIP, n_tiles - 1)]
        save(run_items([(s_ref[i, 0], s_ref[i, 1]) for i in range(LOOKAHEAD)], tiles, after))
        return carry

    assert n_tiles % TILES_PER_TRIP == 0
    lax.fori_loop(0, n_tiles // TILES_PER_TRIP, body, 0)

    heads = []
    for h in range(n_heads):
        acc = acc_ref[h]
        heads.append(acc[:HEAD_DIM] / acc[HEAD_DIM:])
    o_ref[0] = jnp.concatenate(heads, axis=0).T.astype(o_ref.dtype)


def _attn_call(qa_t, ka_c, va_c, ka_l, va_l, *, tq):
    bsz, _, n = qa_t.shape
    n_ctx = ka_c.shape[1]
    n_tiles, tk = ka_l.shape[1], ka_l.shape[2]
    gw = ATTN_WIDTH // ATTN_KV_HEADS
    n_heads = ATTN_HEADS // ATTN_KV_HEADS
    return pl.pallas_call(
        _attn_kernel,
        out_shape=jax.ShapeDtypeStruct((bsz, n, ATTN_WIDTH), BF16),
        grid=(bsz, ATTN_KV_HEADS, n // tq),
        in_specs=[pl.BlockSpec((1, gw, tq), lambda b, g, i: (b, g, i)),
                  pl.BlockSpec((1, n_ctx, 2 * LANES), lambda b, g, i: (b, 0, g)),
                  pl.BlockSpec((1, 1, LANES, n_ctx), lambda b, g, i: (b, 0, g, 0)),
                  pl.BlockSpec((1, n_tiles, tk, 2 * LANES), lambda b, g, i: (b, 0, 0, g)),
                  pl.BlockSpec((1, n_tiles, LANES, tk), lambda b, g, i: (b, 0, g, 0))],
        out_specs=pl.BlockSpec((1, tq, gw), lambda b, g, i: (b, i, g)),
        scratch_shapes=[pltpu.VMEM((n_heads, 1, tq), F32),
                        pltpu.VMEM((n_heads, LANES, tq), F32),
                        pltpu.VMEM((LOOKAHEAD, 2, ITEM_KEYS // 2, tq), F32)],
        compiler_params=pltpu.CompilerParams(
            dimension_semantics=("arbitrary", "arbitrary", "arbitrary"),
            vmem_limit_bytes=VMEM_LIMIT),
        name="gqa_flash_attention",
    )(qa_t, ka_c, va_c, ka_l, va_l)


N_PAIRS = RET_HEADS // 2


def _ret_kernel(lg_head_ref, lg_std_ref, lg_pair_ref,
                qm_ref, km_ref, vm_ref, qb_ref, kb_ref, vb_ref, kc_ref, vc_ref,
                om_ref, ob_ref,
                dsum_ref, qdec_ref, kdec_ref, cdec_ref, state_ref):
    C = RET_CHUNK
    n_ctx = kc_ref.shape[1]
    first_call = (pl.program_id(0) == 0) & (pl.program_id(1) == 0)

    std_a = _lane_iota((C, LANES)) < HEAD_DIM
    pair_a = (_lane_iota((C, LANES)) // HALF) % 2 == 0
    row_head = (lax.broadcasted_iota(jnp.int32, (LANES, LANES), 0) // HALF) % 2
    col_head = lax.broadcasted_iota(jnp.int32, (LANES, LANES), 1) // HEAD_DIM
    same_head = row_head == col_head

    @pl.when(first_call)
    def _tables():
        row = lax.broadcasted_iota(jnp.int32, (C, C), 0).astype(F32)
        colv = lax.broadcasted_iota(jnp.int32, (C, C), 1).astype(F32)
        pos = row[:, :LANES]
        for h in range(RET_HEADS):
            lg_f = -jnp.exp(lg_head_ref[0, h])
            lg_b = -jnp.exp(lg_head_ref[1, h])
            dsum_ref[h] = (jnp.where(row >= colv, jnp.exp(lg_f * jnp.maximum(row - colv, 0.0)), 0.0)
                           + jnp.where(colv >= row, jnp.exp(lg_b * jnp.maximum(colv - row, 0.0)), 0.0))
        for d in range(2):
            for p in range(N_PAIRS):
                lg_std = -jnp.exp(lg_std_ref[d, p])
                lg_pair = -jnp.exp(lg_pair_ref[d, p])
                q_steps = (pos + 1.0) if d == 0 else (C - pos)
                k_steps = (C - 1.0 - pos) if d == 0 else pos
                qdec_ref[d, p] = jnp.exp(lg_std * q_steps)
                kdec_ref[d, p] = jnp.exp(lg_pair * k_steps)
                cdec_ref[d, p] = jnp.exp(lg_std * float(C))

    @pl.when(pl.program_id(1) == 0)
    def _seed_states():
        cpos = lax.broadcasted_iota(jnp.int32, (n_ctx, LANES), 0).astype(F32)
        for d in range(2):
            steps = (n_ctx - 1.0 - cpos) if d == 0 else cpos
            for p in range(N_PAIRS):
                lg_pair = -jnp.exp(lg_pair_ref[d, p])
                kc = kc_ref[0, :, p * LANES:(p + 1) * LANES].astype(F32) * jnp.exp(lg_pair * steps)
                vc = vc_ref[0, :, p * LANES:(p + 1) * LANES]
                state_ref[d, p] = jnp.where(same_head, _dot_tn(kc.astype(BF16), vc), 0.0)

    def decayed(x, dec):
        return (x.astype(F32) * dec).astype(BF16)

    for u in range(RET_CHUNKS_PER_STEP):
        rows_m = slice(u * C, (u + 1) * C)
        rows_b = slice((RET_CHUNKS_PER_STEP - 1 - u) * C, (RET_CHUNKS_PER_STEP - u) * C)

        stage1 = []
        for p in range(N_PAIRS):
            sl = slice(p * LANES, (p + 1) * LANES)
            q, k, v = qm_ref[0, rows_m, sl], km_ref[0, rows_m, sl], vm_ref[0, rows_m, sl]
            q2, k2, v2 = qb_ref[0, rows_b, sl], kb_ref[0, rows_b, sl], vb_ref[0, rows_b, sl]
            zk = jnp.zeros_like(k)
            k_split = jnp.concatenate([jnp.where(pair_a, k, zk), jnp.where(pair_a, zk, k)], axis=0)
            s = _dot_nt(q, k_split)
            cross_f = _dot(q, state_ref[0, p].astype(BF16)) * qdec_ref[0, p]
            cross_b = _dot(q2, state_ref[1, p].astype(BF16)) * qdec_ref[1, p]
            kv_f = _dot_tn(decayed(k, kdec_ref[0, p]), v)
            kv_b = _dot_tn(decayed(k2, kdec_ref[1, p]), v2)
            stage1.append((s, cross_f, cross_b, kv_f, kv_b, v))

        for p, (s, cross_f, cross_b, kv_f, kv_b, v) in enumerate(stage1):
            sl = slice(p * LANES, (p + 1) * LANES)
            w = jnp.concatenate([s[:, :C] * dsum_ref[2 * p], s[:, C:] * dsum_ref[2 * p + 1]], axis=1)
            zv = jnp.zeros_like(v)
            v_split = jnp.concatenate([jnp.where(std_a, v, zv), jnp.where(std_a, zv, v)], axis=0)
            om_ref[0, rows_m, sl] = _dot(w.astype(BF16), v_split) + cross_f
            ob_ref[0, rows_b, sl] = cross_b
            state_ref[0, p] = state_ref[0, p] * cdec_ref[0, p] + jnp.where(same_head, kv_f, 0.0)
            state_ref[1, p] = state_ref[1, p] * cdec_ref[1, p] + jnp.where(same_head, kv_b, 0.0)


def _ret_call(lg_head, lg_std, lg_pair, qr, kr, vr, kr_c, vr_c):
    bsz, n, _ = qr.shape
    n_ctx = kr_c.shape[1]
    C = RET_CHUNK
    rows = C * RET_CHUNKS_PER_STEP
    nc = n // rows
    fwd = pl.BlockSpec((1, rows, RET_WIDTH), lambda b, c: (b, c, 0))
    bwd = pl.BlockSpec((1, rows, RET_WIDTH), lambda b, c: (b, nc - 1 - c, 0))
    ctx = pl.BlockSpec((1, n_ctx, RET_WIDTH), lambda b, c: (b, 0, 0))
    full = lambda a: pl.BlockSpec(a.shape, lambda b, c: (0,) * a.ndim)
    return pl.pallas_call(
        _ret_kernel,
        out_shape=[jax.ShapeDtypeStruct((bsz, n, RET_WIDTH), F32)] * 2,
        grid=(bsz, nc),
        in_specs=[full(lg_head), full(lg_std), full(lg_pair), fwd, fwd, fwd, bwd, bwd, bwd, ctx, ctx],
        out_specs=[fwd, bwd],
        scratch_shapes=[pltpu.VMEM((RET_HEADS, C, C), F32),
                        pltpu.VMEM((2, N_PAIRS, C, LANES), F32),
                        pltpu.VMEM((2, N_PAIRS, C, LANES), F32),
                        pltpu.VMEM((2, N_PAIRS, 1, LANES), F32),
                        pltpu.VMEM((2, N_PAIRS, LANES, LANES), F32)],
        compiler_params=pltpu.CompilerParams(dimension_semantics=("arbitrary", "arbitrary"),
                                             vmem_limit_bytes=VMEM_LIMIT),
        name="bidir_retention",
    )(lg_head, lg_std, lg_pair, qr, kr, vr, qr, kr, vr, kr_c, vr_c)


def _merge_ffn_kernel(x_ref, att_ref, of_ref, ob_ref, gr_ref, mod_ref, gpm_ref, gpf_ref, gqf_ref,
                      seg_ref, wo_ref, wi_ref, w2_ref, out_ref):
    x = x_ref[0]
    gate_m = mod_ref[0, 2:3, :]
    shift_f = mod_ref[0, 3:4, :]
    scale_f = mod_ref[0, 4:5, :]
    gate_f = mod_ref[0, 5:6, :]

    ret = of_ref[0] + ob_ref[0]
    seg = seg_ref[...]
    mu = _head_sums(ret, seg, exact=True) * (1.0 / HEAD_DIM)
    cen = ret - mu
    var = _head_sums(cen * cen, seg) * (1.0 / HEAD_DIM)
    ret_n = cen * lax.rsqrt(var + EPS) * _silu(gr_ref[0])

    mix = (_dot(att_ref[0], wo_ref[:ATTN_WIDTH, :])
           + _dot(ret_n.astype(BF16), wo_ref[ATTN_WIDTH:, :]))
    ms = jnp.mean(mix * mix, axis=-1, keepdims=True)
    x1 = x + gate_m * (mix * lax.rsqrt(ms + EPS) * gpm_ref[...])

    ms1 = jnp.mean(x1 * x1, axis=-1, keepdims=True)
    hf = ((x1 * lax.rsqrt(ms1 + EPS) * gpf_ref[...]) * (1.0 + scale_f) + shift_f).astype(BF16)

    acc = jnp.zeros(x.shape, F32)
    for j in range(FFN_HIDDEN // FFN_CHUNK):
        lo = j * FFN_CHUNK
        a = _dot(hf, wi_ref[:, lo:lo + FFN_CHUNK])
        b = _dot(hf, wi_ref[:, FFN_HIDDEN + lo:FFN_HIDDEN + lo + FFN_CHUNK])
        acc = acc + _dot((_silu(a) * b).astype(BF16), w2_ref[lo:lo + FFN_CHUNK, :])
    ms2 = jnp.mean(acc * acc, axis=-1, keepdims=True)
    out_ref[0] = x1 + gate_f * (acc * lax.rsqrt(ms2 + EPS) * gqf_ref[...])


def _merge_ffn_call(x, att, o_f, o_b, gr, mod, g_post_mix, g_pre_ffn, g_post_ffn, seg, w_out,
                    w_ffn_in, w_ffn_out, *, tm):
    bsz, n, _ = x.shape
    tok = lambda width: pl.BlockSpec((1, tm, width), lambda b, i: (b, i, 0))
    full = lambda a: pl.BlockSpec(a.shape, lambda b, i: (0,) * a.ndim,
                                  pipeline_mode=pl.Buffered(1))
    return pl.pallas_call(
        _merge_ffn_kernel,
        out_shape=jax.ShapeDtypeStruct(x.shape, F32),
        grid=(bsz, n // tm),
        in_specs=[tok(D_MODEL), tok(ATTN_WIDTH), tok(RET_WIDTH), tok(RET_WIDTH), tok(RET_WIDTH),
                  pl.BlockSpec((1, N_MOD, D_MODEL), lambda b, i: (b, 0, 0)),
                  full(g_post_mix), full(g_pre_ffn), full(g_post_ffn), full(seg),
                  full(w_out), full(w_ffn_in), full(w_ffn_out)],
        out_specs=tok(D_MODEL),
        compiler_params=pltpu.CompilerParams(dimension_semantics=("arbitrary", "arbitrary"),
                                             vmem_limit_bytes=VMEM_LIMIT),
        name="merge_outproj_ffn",
    )(x, att, o_f, o_b, gr, mod, g_post_mix, g_pre_ffn, g_post_ffn, seg, w_out, w_ffn_in, w_ffn_out)


def _rope_tables(n_lat):
    rows = n_lat // GRID_W
    n_freq = HEAD_DIM // 4
    inv = ROPE_BASE ** (-jnp.arange(n_freq, dtype=F32) / n_freq)
    ang_row = jnp.arange(rows, dtype=F32)[:, None] * inv
    ang_col = jnp.arange(GRID_W, dtype=F32)[:, None] * inv

    def table(fn, signed):
        per_token = jnp.concatenate(
            [jnp.broadcast_to(fn(ang_row)[:, None, :], (rows, GRID_W, n_freq)),
             jnp.broadcast_to(fn(ang_col)[None, :, :], (rows, GRID_W, n_freq))], axis=-1)
        per_token = per_token.reshape(n_lat, HALF)
        halves = [-per_token, -per_token, per_token, per_token] if signed else [per_token] * 4
        return jnp.concatenate(halves, axis=-1)

    return table(jnp.cos, False), table(jnp.sin, True)


def _segment_matrix(width, same):
    i = np.arange(width)
    return jnp.asarray(same(i[:, None], i[None, :]), BF16)


def kernel(x, c, ctx, c_ctx, w_mod, b_mod, g_pre_mix, g_post_mix, g_pre_ffn, g_post_ffn,
           w_in, q_norm_g, k_norm_g, ret_decay_fwd, ret_decay_bwd, w_out, w_ffn_in, w_ffn_out):
    bsz, n_lat, _ = x.shape
    assert w_mod.shape[0] == 1, "single-layer block"
    layer = 0

    w_proj = jnp.take(w_in[layer], _PROJ_GATHER, axis=1, mode="fill", fill_value=0).astype(BF16)
    gq_tab = q_norm_g[layer][_PAIR_DIM].reshape(1, LANES)
    gk_tab = k_norm_g[layer][_PAIR_DIM].reshape(1, LANES)
    seg_pair = _segment_matrix(ATTN_WIDTH, lambda i, j: (i // LANES == j // LANES)
                               & ((i // HALF) % 2 == (j // HALF) % 2))
    seg_std = _segment_matrix(RET_WIDTH, lambda i, j: i // HEAD_DIM == j // HEAD_DIM)
    cos_t, sin_t = _rope_tables(n_lat)

    decay = jnp.stack([ret_decay_fwd[layer], ret_decay_bwd[layer]]).astype(F32)
    lg_head = jnp.broadcast_to(decay[:, :, None, None], (2, RET_HEADS, 1, LANES))
    std_head = np.arange(RET_WIDTH) // HEAD_DIM
    lg_std = decay[:, std_head].reshape(2, N_PAIRS, 1, LANES)
    lg_pair = decay[:, _PAIR8 // HEAD_DIM].reshape(2, N_PAIRS, 1, LANES)

    w_ffn_in_b = w_ffn_in[layer].astype(BF16)
    w_ffn_out_b = w_ffn_out[layer].astype(BF16)
    w_out_b = w_out[layer].astype(BF16)

    c_rows = jnp.concatenate([c, c_ctx[None, :], jnp.zeros((8 - bsz - 1, D_MODEL), c.dtype)], axis=0)
    mod = _mod_call(c_rows, w_mod[layer], b_mod[layer]).reshape(8, N_MOD, D_MODEL)

    g_pre = g_pre_mix[layer].reshape(1, D_MODEL)
    qa, ka_l, va_l, qr, kr, vr, gr = _proj_call(x, mod, g_pre, w_proj, gq_tab, gk_tab, seg_pair,
                                                cos_t, sin_t, latent=True, tm=512)
    ka_c, va_c, kr_c, vr_c = _proj_call(ctx, mod[bsz:bsz + 1], g_pre, w_proj, None, gk_tab, None,
                                        None, None, latent=False, tm=ctx.shape[1])

    tk = va_l.shape[-1]
    att = _attn_call(qa, ka_c, va_c, ka_l.reshape(bsz, n_lat // tk, tk, 4 * KV_WIDTH), va_l, tq=256)
    o_f, o_b = _ret_call(lg_head, lg_std, lg_pair, qr, kr, vr, kr_c, vr_c)

    return _merge_ffn_call(x, att, o_f, o_b, gr, mod,
                           g_post_mix[layer].reshape(1, D_MODEL),
                           g_pre_ffn[layer].reshape(1, D_MODEL),
                           g_post_ffn[layer].reshape(1, D_MODEL),
                           seg_std, w_out_b, w_ffn_in_b, w_ffn_out_b, tm=512)
```

```python
import functools
import math

import numpy as np
import jax
import jax.numpy as jnp
from jax import lax
from jax.experimental import pallas as pl
from jax.experimental.pallas import tpu as pltpu

D_MODEL = 1024
HEAD_DIM = 64
HALF = HEAD_DIM // 2
GRID_W = 64
ATTN_HEADS = 8
ATTN_KV_HEADS = 2
RET_HEADS = 8
ATTN_WIDTH = ATTN_HEADS * HEAD_DIM
KV_WIDTH = ATTN_KV_HEADS * HEAD_DIM
RET_WIDTH = RET_HEADS * HEAD_DIM
FFN_HIDDEN = 2816
FFN_CHUNK = 256
N_MOD = 6
RET_CHUNK = 128
RET_CHUNKS_PER_STEP = 2
ROPE_BASE = 10000.0
ATTN_SCALE = HEAD_DIM ** -0.5
EPS = 1e-6
LOG2E = math.log2(math.e)

LANES = 128
MXU_DEPTH = 256
ONES_ROWS = 16
V_ROWS = HEAD_DIM + ONES_ROWS
VMEM_LIMIT = 56 * 1024 * 1024

F32 = jnp.float32
BF16 = jnp.bfloat16

_QA0 = 0
_KA0 = ATTN_WIDTH
_VA0 = ATTN_WIDTH + KV_WIDTH
_QR0 = ATTN_WIDTH + 2 * KV_WIDTH
_KR0 = _QR0 + RET_WIDTH
_VR0 = _KR0 + RET_WIDTH
_GR0 = _VR0 + RET_WIDTH


def _pair_perm(n_heads):
    idx = []
    for p in range(n_heads // 2):
        for seg in range(4):
            head = 2 * p + (seg % 2)
            lo = (seg // 2) * HALF
            idx.extend(head * HEAD_DIM + lo + d for d in range(HALF))
    return np.asarray(idx, np.int32)


_PAIR8 = _pair_perm(8)


def _pair_layout(w):
    rows, width = w.shape
    return (w.reshape(rows, width // LANES, 2, 2, HALF).transpose(0, 1, 3, 2, 4)
            .reshape(rows, width))


def _projection_weights(w):
    rows = w.shape[0]
    ka = w[:, _KA0:_KA0 + KV_WIDTH].reshape(rows, ATTN_KV_HEADS, 2, 1, HALF)
    ka = jnp.broadcast_to(ka, (rows, ATTN_KV_HEADS, 2, 2, HALF)).reshape(rows, 2 * KV_WIDTH)
    va = w[:, _VA0:_VA0 + KV_WIDTH]
    groups = [("qa", _pair_layout(w[:, _QA0:_QA0 + ATTN_WIDTH])), ("ka", ka), ("va", va),
              ("qr", _pair_layout(w[:, _QR0:_QR0 + RET_WIDTH])),
              ("kr", _pair_layout(w[:, _KR0:_KR0 + RET_WIDTH])),
              ("vr", w[:, _VR0:_VR0 + RET_WIDTH]), ("gr", w[:, _GR0:_GR0 + RET_WIDTH])]
    return jnp.concatenate([g for _, g in groups], axis=1)


_PROJ_COLS = {"qa": (0, ATTN_WIDTH), "ka": (ATTN_WIDTH, 2 * KV_WIDTH),
              "va": (ATTN_WIDTH + 2 * KV_WIDTH, KV_WIDTH)}
for _i, _name in enumerate(("qr", "kr", "vr", "gr")):
    _PROJ_COLS[_name] = (ATTN_WIDTH + 3 * KV_WIDTH + _i * RET_WIDTH, RET_WIDTH)
_PAIR_DIM = (_PAIR8[:LANES] % HEAD_DIM).astype(np.int32)


def _dot(a, b):
    return jnp.dot(a, b, preferred_element_type=F32)


def _dot_nt(a, b):
    return lax.dot_general(a, b, (((1,), (1,)), ((), ())), preferred_element_type=F32)


def _dot_tn(a, b):
    return lax.dot_general(a, b, (((0,), (0,)), ((), ())), preferred_element_type=F32)


def _head_sums(a, same_head_bf16, exact=False):
    hi = a.astype(BF16)
    out = _dot(hi, same_head_bf16)
    if exact:
        out = out + _dot((a - hi.astype(F32)).astype(BF16), same_head_bf16)
    return out


def _silu(x):
    return x * (1.0 / (1.0 + jnp.exp(-x)))


def _lane_iota(shape):
    return lax.broadcasted_iota(jnp.int32, shape, len(shape) - 1)


def _mod_kernel(c_ref, w_ref, b_ref, o_ref):
    h = _silu(c_ref[...]).astype(BF16)
    o_ref[...] = _dot(h, w_ref[...].astype(BF16)) + b_ref[...]


def _mod_call(c_rows, w_mod, b_mod):
    rows = c_rows.shape[0]
    n_out = w_mod.shape[1]
    tn = D_MODEL
    return pl.pallas_call(
        _mod_kernel,
        out_shape=jax.ShapeDtypeStruct((rows, n_out), F32),
        grid=(n_out // tn,),
        in_specs=[
            pl.BlockSpec((rows, D_MODEL), lambda j: (0, 0)),
            pl.BlockSpec((D_MODEL, tn), lambda j: (0, j)),
            pl.BlockSpec((1, tn), lambda j: (0, j)),
        ],
        out_specs=pl.BlockSpec((rows, tn), lambda j: (0, j)),
        compiler_params=pltpu.CompilerParams(dimension_semantics=("arbitrary",),
                                             vmem_limit_bytes=VMEM_LIMIT),
        name="adaln_mod",
    )(c_rows, w_mod, b_mod.reshape(1, n_out))


def _rope(y, cos_t, sin_t):
    blocks = []
    for p in range(y.shape[1] // LANES):
        yp = y[:, p * LANES:(p + 1) * LANES]
        blocks.append(yp * cos_t + pltpu.roll(yp, LANES // 2, 1) * sin_t)
    return blocks[0] if len(blocks) == 1 else jnp.concatenate(blocks, axis=1)


def _proj_kernel(*refs, latent):
    if latent:
        (x_ref, mod_ref, g_ref, w_ref, gq_ref, gk_ref, seg_ref, cos_ref, sin_ref,
         qa_ref, ka_ref, va_ref, qr_ref, kr_ref, vr_ref, gr_ref) = refs
    else:
        (x_ref, mod_ref, g_ref, w_ref, gk_ref,
         ka_ref, va_ref, kr_ref, vr_ref) = refs

    x = x_ref[0]
    ms = jnp.mean(x * x, axis=-1, keepdims=True)
    shift = mod_ref[0, 0:1, :]
    scale = mod_ref[0, 1:2, :]
    h = (x * lax.rsqrt(ms + EPS) * g_ref[...]) * (1.0 + scale) + shift
    hb = h.astype(BF16)

    def proj(name):
        start, width = _PROJ_COLS[name]
        return _dot(hb, w_ref[:, start:start + width])

    if latent:
        cos_t = cos_ref[...]
        sin_t = sin_ref[...]
        rope = lambda y: _rope(y, cos_t, sin_t)
        yq = proj("qa")
        ssq = _head_sums(yq * yq, seg_ref[...])
        gq = jnp.concatenate([gq_ref[...]] * (ATTN_WIDTH // LANES), axis=1)
        qn = yq * lax.rsqrt(ssq * (1.0 / HEAD_DIM) + EPS) * gq
        qa_ref[0] = (rope(qn) * (ATTN_SCALE * LOG2E)).T.astype(BF16)
    else:
        rope = lambda y: y

    yk = proj("ka")
    slot_b = (_lane_iota((yk.shape[0], LANES)) // HALF) % 2 == 1
    kblocks = []
    for kv in range(ATTN_KV_HEADS):
        ykv = yk[:, kv * LANES:(kv + 1) * LANES]
        ssq = 0.5 * jnp.sum(ykv * ykv, axis=-1, keepdims=True)
        kn = rope(ykv * lax.rsqrt(ssq * (1.0 / HEAD_DIM) + EPS) * gk_ref[...])
        kblocks += [jnp.where(slot_b, 0.0, kn), jnp.where(slot_b, kn, 0.0)]
    ka_ref[0] = jnp.concatenate(kblocks, axis=1).astype(BF16)

    yv_t = proj("va").T
    ones = jnp.ones((ONES_ROWS, yv_t.shape[1]), F32)
    va_ref[0, 0] = jnp.concatenate(
        [blk for kv in range(ATTN_KV_HEADS)
         for blk in (yv_t[kv * HEAD_DIM:(kv + 1) * HEAD_DIM], ones)], axis=0).astype(BF16)

    if latent:
        qr_ref[0] = rope(proj("qr")).astype(BF16)
    kr_ref[0] = rope(proj("kr") * ATTN_SCALE).astype(BF16)
    vr_ref[0] = proj("vr").astype(BF16)
    if latent:
        gr_ref[0] = proj("gr")


def _proj_call(x, mod, g_pre, w, gq_tab, gk_tab, seg, cos_t, sin_t, *, latent, tm):
    bsz, n, _ = x.shape
    wcols = w.shape[1]
    tok = lambda width: pl.BlockSpec((1, tm, width), lambda b, i: (b, i, 0))
    full = lambda shape: pl.BlockSpec(shape, lambda b, i: (0,) * len(shape))
    mod_spec = pl.BlockSpec((1, N_MOD, D_MODEL), (lambda b, i: (b, 0, 0)) if latent
                            else (lambda b, i: (0, 0, 0)))
    if latent:
        in_specs = [tok(D_MODEL), mod_spec, full((1, D_MODEL)), full((D_MODEL, wcols)),
                    full((1, LANES)), full((1, LANES)), full((ATTN_WIDTH, ATTN_WIDTH)),
                    pl.BlockSpec((tm, LANES), lambda b, i: (i, 0)),
                    pl.BlockSpec((tm, LANES), lambda b, i: (i, 0))]
        args = (x, mod, g_pre, w, gq_tab, gk_tab, seg, cos_t, sin_t)
    else:
        in_specs = [tok(D_MODEL), mod_spec, full((1, D_MODEL)), full((D_MODEL, wcols)),
                    full((1, LANES))]
        args = (x, mod, g_pre, w, gk_tab)
    token_major = lambda width, dt: (jax.ShapeDtypeStruct((bsz, n, width), dt), tok(width))
    q_t = (jax.ShapeDtypeStruct((bsz, ATTN_WIDTH, n), BF16),
           pl.BlockSpec((1, ATTN_WIDTH, tm), lambda b, i: (b, 0, i)))
    k_ab = token_major(4 * KV_WIDTH, BF16)
    v_t = (jax.ShapeDtypeStruct((bsz, n // tm, ATTN_KV_HEADS * V_ROWS, tm), BF16),
           pl.BlockSpec((1, 1, ATTN_KV_HEADS * V_ROWS, tm), lambda b, i: (b, i, 0, 0)))
    ret = token_major(RET_WIDTH, BF16)
    outs = ([q_t, k_ab, v_t, ret, ret, ret, token_major(RET_WIDTH, F32)] if latent
            else [k_ab, v_t, ret, ret])
    return pl.pallas_call(
        functools.partial(_proj_kernel, latent=latent),
        out_shape=[o[0] for o in outs],
        grid=(bsz, n // tm),
        in_specs=in_specs,
        out_specs=[o[1] for o in outs],
        compiler_params=pltpu.CompilerParams(dimension_semantics=("arbitrary", "arbitrary"),
                                             vmem_limit_bytes=VMEM_LIMIT),
        name="in_proj_latent" if latent else "in_proj_ctx",
    )(*args)


LOOKAHEAD = 3
TILES_PER_TRIP = 8
ITEM_KEYS = 512
DRAIN_ITEMS = 2
Q_SPLITS = 2


def _attn_kernel(qt_ref, kc_ref, vc_ref, kl_ref, vl_ref, o_ref, m_ref, acc_ref, s_ref):
    n_heads = 2 * (qt_ref.shape[1] // LANES)
    tq = qt_ref.shape[2] // Q_SPLITS
    n_streams = n_heads * Q_SPLITS

    m_ref[...] = jnp.full(m_ref.shape, -jnp.inf, F32)
    acc_ref[...] = jnp.zeros(acc_ref.shape, F32)

    n_tiles = kl_ref.shape[1]

    def scores(k_ref, item):
        sub, stream = divmod(item, n_streams)
        qb, h = divmod(stream, n_heads)
        q_t = qt_ref[0, (h // 2) * LANES:(h // 2 + 1) * LANES, qb * tq:(qb + 1) * tq]
        keys = min(ITEM_KEYS, k_ref.shape[0])
        half = keys // 2
        lo = sub * keys
        cols = slice((h % 2) * LANES, (h % 2 + 1) * LANES)
        return (_dot(k_ref[lo:lo + half, cols], q_t), _dot(k_ref[lo + half:lo + keys, cols], q_t))

    def accumulate(item, s, v_ref):
        sub, h = divmod(item, n_streams)
        half = s[0].shape[0]
        lo = sub * 2 * half
        m_old = m_ref[h]
        m_new = jnp.maximum(m_old, jnp.maximum(jnp.max(s[0], axis=0, keepdims=True),
                                               jnp.max(s[1], axis=0, keepdims=True)))
        alpha = jnp.exp2(m_old - m_new)
        p0 = jnp.exp2(s[0] - m_new).astype(BF16)
        p1 = jnp.exp2(s[1] - m_new).astype(BF16)
        if 2 * half <= MXU_DEPTH:
            pv = _dot(v_ref[:, lo:lo + 2 * half], jnp.concatenate([p0, p1], axis=0))
        else:
            pv = _dot(v_ref[:, lo:lo + half], p0) + _dot(v_ref[:, lo + half:lo + 2 * half], p1)
        acc_ref[h] = alpha * acc_ref[h] + pv
        m_ref[h] = m_new

    def run_items(s_ahead, tiles, after):
        per_tile = [n_streams * max(1, k_ref.shape[0] // ITEM_KEYS) for k_ref, _ in tiles]
        refs = [(k_ref, v_ref, i) for (k_ref, v_ref), n in zip(tiles, per_tile) for i in range(n)]
        refs_after = [(after, None, i) for i in range(LOOKAHEAD)]
        total = len(refs)
        requests = [[i + LOOKAHEAD] for i in range(total)]
        drain = min(DRAIN_ITEMS, total // 2)
        for i in range(total - drain, total):
            requests[i - drain] += requests[i]
            requests[i] = []
        pending = dict(enumerate(s_ahead))
        for i, (k_ref, v_ref, item) in enumerate(refs):
            for r in requests[i]:
                rk, _, ritem = (refs + refs_after)[r]
                pending[r] = scores(rk, ritem)
            accumulate(item, pending.pop(i), v_ref)
        return [pending[total + j] for j in range(LOOKAHEAD)]

    def save(s_ahead):
        for i, s in enumerate(s_ahead):
            s_ref[i, 0] = s[0]
            s_ref[i, 1] = s[1]

    ctx_k = kc_ref.at[0]
    save(run_items([scores(ctx_k, i) for i in range(LOOKAHEAD)], [(ctx_k, vc_ref.at[0, 0])],
                   kl_ref.at[0, 0]))

    def body(jj, carry):
        j0 = jj * TILES_PER_TRIP
        tiles = [(kl_ref.at[0, j0 + u], vl_ref.at[0, j0 + u]) for u in range(TILES_PER_TRIP)]
        after = kl_ref.at[0, jnp.minimum(j0 + TILES_PER_TRIP, n_tiles - 1)]
        save(run_items([(s_ref[i, 0], s_ref[i, 1]) for i in range(LOOKAHEAD)], tiles, after))
        return carry

    assert n_tiles % TILES_PER_TRIP == 0
    lax.fori_loop(0, n_tiles // TILES_PER_TRIP, body, 0)

    for qb in range(Q_SPLITS):
        heads = []
        for h in range(n_heads):
            acc = acc_ref[qb * n_heads + h]
            heads.append(acc[:HEAD_DIM] / acc[HEAD_DIM:HEAD_DIM + 1])
        o_ref[0, qb * tq:(qb + 1) * tq, :] = jnp.concatenate(heads, axis=0).T.astype(o_ref.dtype)


def _attn_call(qa_t, ka_c, va_c, ka_l, va_l, *, tq):
    bsz, _, n = qa_t.shape
    n_ctx = ka_c.shape[1]
    n_tiles, tk = ka_l.shape[1], ka_l.shape[2]
    gw = ATTN_WIDTH // ATTN_KV_HEADS
    n_heads = ATTN_HEADS // ATTN_KV_HEADS
    return pl.pallas_call(
        _attn_kernel,
        out_shape=jax.ShapeDtypeStruct((bsz, n, ATTN_WIDTH), BF16),
        grid=(bsz, ATTN_KV_HEADS, n // tq),
        in_specs=[pl.BlockSpec((1, gw, tq), lambda b, g, i: (b, g, i)),
                  pl.BlockSpec((1, n_ctx, 2 * LANES), lambda b, g, i: (b, 0, g)),
                  pl.BlockSpec((1, 1, V_ROWS, n_ctx), lambda b, g, i: (b, 0, g, 0)),
                  pl.BlockSpec((1, n_tiles, tk, 2 * LANES), lambda b, g, i: (b, 0, 0, g)),
                  pl.BlockSpec((1, n_tiles, V_ROWS, tk), lambda b, g, i: (b, 0, g, 0))],
        out_specs=pl.BlockSpec((1, tq, gw), lambda b, g, i: (b, i, g)),
        scratch_shapes=[pltpu.VMEM((n_heads * Q_SPLITS, 1, tq // Q_SPLITS), F32),
                        pltpu.VMEM((n_heads * Q_SPLITS, V_ROWS, tq // Q_SPLITS), F32),
                        pltpu.VMEM((LOOKAHEAD, 2, ITEM_KEYS // 2, tq // Q_SPLITS), F32)],
        compiler_params=pltpu.CompilerParams(
            dimension_semantics=("arbitrary", "arbitrary", "arbitrary"),
            vmem_limit_bytes=VMEM_LIMIT),
        name="gqa_flash_attention",
    )(qa_t, ka_c, va_c, ka_l, va_l)


N_PAIRS = RET_HEADS // 2


def _ret_kernel(lg_head_ref, lg_std_ref, lg_pair_ref,
                qm_ref, km_ref, vm_ref, qb_ref, kb_ref, vb_ref, kc_ref, vc_ref,
                om_ref, ob_ref,
                dsum_ref, qdec_ref, kdec_ref, cdec_ref, state_ref):
    C = RET_CHUNK
    n_ctx = kc_ref.shape[1]
    first_call = (pl.program_id(0) == 0) & (pl.program_id(1) == 0)

    std_a = _lane_iota((C, LANES)) < HEAD_DIM
    pair_a = (_lane_iota((C, LANES)) // HALF) % 2 == 0
    row_head = (lax.broadcasted_iota(jnp.int32, (LANES, LANES), 0) // HALF) % 2
    col_head = lax.broadcasted_iota(jnp.int32, (LANES, LANES), 1) // HEAD_DIM
    same_head = row_head == col_head

    @pl.when(first_call)
    def _tables():
        row = lax.broadcasted_iota(jnp.int32, (C, C), 0).astype(F32)
        colv = lax.broadcasted_iota(jnp.int32, (C, C), 1).astype(F32)
        pos = row[:, :LANES]
        for h in range(RET_HEADS):
            lg_f = -jnp.exp(lg_head_ref[0, h])
            lg_b = -jnp.exp(lg_head_ref[1, h])
            dsum_ref[h] = (jnp.where(row >= colv, jnp.exp(lg_f * jnp.maximum(row - colv, 0.0)), 0.0)
                           + jnp.where(colv >= row, jnp.exp(lg_b * jnp.maximum(colv - row, 0.0)), 0.0))
        for d in range(2):
            for p in range(N_PAIRS):
                lg_std = -jnp.exp(lg_std_ref[d, p])
                lg_pair = -jnp.exp(lg_pair_ref[d, p])
                q_steps = (pos + 1.0) if d == 0 else (C - pos)
                k_steps = (C - 1.0 - pos) if d == 0 else pos
                qdec_ref[d, p] = jnp.exp(lg_std * q_steps)
                kdec_ref[d, p] = jnp.exp(lg_pair * k_steps)
                cdec_ref[d, p] = jnp.exp(lg_std * float(C))

    @pl.when(pl.program_id(1) == 0)
    def _seed_states():
        cpos = lax.broadcasted_iota(jnp.int32, (n_ctx, LANES), 0).astype(F32)
        for d in range(2):
            steps = (n_ctx - 1.0 - cpos) if d == 0 else cpos
            for p in range(N_PAIRS):
                lg_pair = -jnp.exp(lg_pair_ref[d, p])
                kc = kc_ref[0, :, p * LANES:(p + 1) * LANES].astype(F32) * jnp.exp(lg_pair * steps)
                vc = vc_ref[0, :, p * LANES:(p + 1) * LANES]
                state_ref[d, p] = jnp.where(same_head, _dot_tn(kc.astype(BF16), vc), 0.0)

    def decayed(x, dec):
        return (x.astype(F32) * dec).astype(BF16)

    for u in range(RET_CHUNKS_PER_STEP):
        rows_m = slice(u * C, (u + 1) * C)
        rows_b = slice((RET_CHUNKS_PER_STEP - 1 - u) * C, (RET_CHUNKS_PER_STEP - u) * C)

        stage1 = []
        for p in range(N_PAIRS):
            sl = slice(p * LANES, (p + 1) * LANES)
            q, k, v = qm_ref[0, rows_m, sl], km_ref[0, rows_m, sl], vm_ref[0, rows_m, sl]
            q2, k2, v2 = qb_ref[0, rows_b, sl], kb_ref[0, rows_b, sl], vb_ref[0, rows_b, sl]
            zk = jnp.zeros_like(k)
            k_split = jnp.concatenate([jnp.where(pair_a, k, zk), jnp.where(pair_a, zk, k)], axis=0)
            s = _dot_nt(q, k_split)
            cross_f = _dot(q, state_ref[0, p].astype(BF16)) * qdec_ref[0, p]
            cross_b = _dot(q2, state_ref[1, p].astype(BF16)) * qdec_ref[1, p]
            kv_f = _dot_tn(decayed(k, kdec_ref[0, p]), v)
            kv_b = _dot_tn(decayed(k2, kdec_ref[1, p]), v2)
            stage1.append((s, cross_f, cross_b, kv_f, kv_b, v))

        for p, (s, cross_f, cross_b, kv_f, kv_b, v) in enumerate(stage1):
            sl = slice(p * LANES, (p + 1) * LANES)
            w = jnp.concatenate([s[:, :C] * dsum_ref[2 * p], s[:, C:] * dsum_ref[2 * p + 1]], axis=1)
            zv = jnp.zeros_like(v)
            v_split = jnp.concatenate([jnp.where(std_a, v, zv), jnp.where(std_a, zv, v)], axis=0)
            om_ref[0, rows_m, sl] = _dot(w.astype(BF16), v_split) + cross_f
            ob_ref[0, rows_b, sl] = cross_b
            state_ref[0, p] = state_ref[0, p] * cdec_ref[0, p] + jnp.where(same_head, kv_f, 0.0)
            state_ref[1, p] = state_ref[1, p] * cdec_ref[1, p] + jnp.where(same_head, kv_b, 0.0)


def _ret_call(lg_head, lg_std, lg_pair, qr, kr, vr, kr_c, vr_c):
    bsz, n, _ = qr.shape
    n_ctx = kr_c.shape[1]
    C = RET_CHUNK
    rows = C * RET_CHUNKS_PER_STEP
    nc = n // rows
    fwd = pl.BlockSpec((1, rows, RET_WIDTH), lambda b, c: (b, c, 0))
    bwd = pl.BlockSpec((1, rows, RET_WIDTH), lambda b, c: (b, nc - 1 - c, 0))
    ctx = pl.BlockSpec((1, n_ctx, RET_WIDTH), lambda b, c: (b, 0, 0))
    full = lambda a: pl.BlockSpec(a.shape, lambda b, c: (0,) * a.ndim)
    return pl.pallas_call(
        _ret_kernel,
        out_shape=[jax.ShapeDtypeStruct((bsz, n, RET_WIDTH), F32)] * 2,
        grid=(bsz, nc),
        in_specs=[full(lg_head), full(lg_std), full(lg_pair), fwd, fwd, fwd, bwd, bwd, bwd, ctx, ctx],
        out_specs=[fwd, bwd],
        scratch_shapes=[pltpu.VMEM((RET_HEADS, C, C), F32),
                        pltpu.VMEM((2, N_PAIRS, C, LANES), F32),
                        pltpu.VMEM((2, N_PAIRS, C, LANES), F32),
                        pltpu.VMEM((2, N_PAIRS, 1, LANES), F32),
                        pltpu.VMEM((2, N_PAIRS, LANES, LANES), F32)],
        compiler_params=pltpu.CompilerParams(dimension_semantics=("arbitrary", "arbitrary"),
                                             vmem_limit_bytes=VMEM_LIMIT),
        name="bidir_retention",
    )(lg_head, lg_std, lg_pair, qr, kr, vr, qr, kr, vr, kr_c, vr_c)


def _merge_ffn_kernel(x_ref, att_ref, of_ref, ob_ref, gr_ref, mod_ref, gpm_ref, gpf_ref, gqf_ref,
                      seg_ref, wo_ref, wi_ref, w2_ref, out_ref):
    x = x_ref[0]
    gate_m = mod_ref[0, 2:3, :]
    shift_f = mod_ref[0, 3:4, :]
    scale_f = mod_ref[0, 4:5, :]
    gate_f = mod_ref[0, 5:6, :]

    ret = of_ref[0] + ob_ref[0]
    seg = seg_ref[...]
    mu = _head_sums(ret, seg, exact=True) * (1.0 / HEAD_DIM)
    cen = ret - mu
    var = _head_sums(cen * cen, seg) * (1.0 / HEAD_DIM)
    ret_n = cen * lax.rsqrt(var + EPS) * _silu(gr_ref[0])

    mix = (_dot(att_ref[0], wo_ref[:ATTN_WIDTH, :])
           + _dot(ret_n.astype(BF16), wo_ref[ATTN_WIDTH:, :]))
    ms = jnp.mean(mix * mix, axis=-1, keepdims=True)
    x1 = x + gate_m * (mix * lax.rsqrt(ms + EPS) * gpm_ref[...])

    ms1 = jnp.mean(x1 * x1, axis=-1, keepdims=True)
    hf = ((x1 * lax.rsqrt(ms1 + EPS) * gpf_ref[...]) * (1.0 + scale_f) + shift_f).astype(BF16)

    acc = jnp.zeros(x.shape, F32)
    for j in range(FFN_HIDDEN // FFN_CHUNK):
        lo = j * FFN_CHUNK
        a = _dot(hf, wi_ref[:, lo:lo + FFN_CHUNK])
        b = _dot(hf, wi_ref[:, FFN_HIDDEN + lo:FFN_HIDDEN + lo + FFN_CHUNK])
        acc = acc + _dot((_silu(a) * b).astype(BF16), w2_ref[lo:lo + FFN_CHUNK, :])
    ms2 = jnp.mean(acc * acc, axis=-1, keepdims=True)
    out_ref[0] = x1 + gate_f * (acc * lax.rsqrt(ms2 + EPS) * gqf_ref[...])


def _merge_ffn_call(x, att, o_f, o_b, gr, mod, g_post_mix, g_pre_ffn, g_post_ffn, seg, w_out,
                    w_ffn_in, w_ffn_out, *, tm):
    bsz, n, _ = x.shape
    tok = lambda width: pl.BlockSpec((1, tm, width), lambda b, i: (b, i, 0))
    full = lambda a: pl.BlockSpec(a.shape, lambda b, i: (0,) * a.ndim,
                                  pipeline_mode=pl.Buffered(1))
    return pl.pallas_call(
        _merge_ffn_kernel,
        out_shape=jax.ShapeDtypeStruct(x.shape, F32),
        grid=(bsz, n // tm),
        in_specs=[tok(D_MODEL), tok(ATTN_WIDTH), tok(RET_WIDTH), tok(RET_WIDTH), tok(RET_WIDTH),
                  pl.BlockSpec((1, N_MOD, D_MODEL), lambda b, i: (b, 0, 0)),
                  full(g_post_mix), full(g_pre_ffn), full(g_post_ffn), full(seg),
                  full(w_out), full(w_ffn_in), full(w_ffn_out)],
        out_specs=tok(D_MODEL),
        compiler_params=pltpu.CompilerParams(dimension_semantics=("arbitrary", "arbitrary"),
                                             vmem_limit_bytes=VMEM_LIMIT),
        name="merge_outproj_ffn",
    )(x, att, o_f, o_b, gr, mod, g_post_mix, g_pre_ffn, g_post_ffn, seg, w_out, w_ffn_in, w_ffn_out)


def _rope_tables(n_lat):
    rows = n_lat // GRID_W
    n_freq = HEAD_DIM // 4
    inv = ROPE_BASE ** (-jnp.arange(n_freq, dtype=F32) / n_freq)
    ang_row = jnp.arange(rows, dtype=F32)[:, None] * inv
    ang_col = jnp.arange(GRID_W, dtype=F32)[:, None] * inv

    def table(fn, signed):
        per_token = jnp.concatenate(
            [jnp.broadcast_to(fn(ang_row)[:, None, :], (rows, GRID_W, n_freq)),
             jnp.broadcast_to(fn(ang_col)[None, :, :], (rows, GRID_W, n_freq))], axis=-1)
        per_token = per_token.reshape(n_lat, HALF)
        halves = [-per_token, -per_token, per_token, per_token] if signed else [per_token] * 4
        return jnp.concatenate(halves, axis=-1)

    return table(jnp.cos, False), table(jnp.sin, True)


def _segment_matrix(width, same):
    i = np.arange(width)
    return jnp.asarray(same(i[:, None], i[None, :]), BF16)


def kernel(x, c, ctx, c_ctx, w_mod, b_mod, g_pre_mix, g_post_mix, g_pre_ffn, g_post_ffn,
           w_in, q_norm_g, k_norm_g, ret_decay_fwd, ret_decay_bwd, w_out, w_ffn_in, w_ffn_out):
    bsz, n_lat, _ = x.shape
    assert w_mod.shape[0] == 1, "single-layer block"
    layer = 0

    w_proj = _projection_weights(w_in[layer]).astype(BF16)
    gq_tab = q_norm_g[layer][_PAIR_DIM].reshape(1, LANES)
    gk_tab = k_norm_g[layer][_PAIR_DIM].reshape(1, LANES)
    seg_pair = _segment_matrix(ATTN_WIDTH, lambda i, j: (i // LANES == j // LANES)
                               & ((i // HALF) % 2 == (j // HALF) % 2))
    seg_std = _segment_matrix(RET_WIDTH, lambda i, j: i // HEAD_DIM == j // HEAD_DIM)
    cos_t, sin_t = _rope_tables(n_lat)

    decay = jnp.stack([ret_decay_fwd[layer], ret_decay_bwd[layer]]).astype(F32)
    lg_head = jnp.broadcast_to(decay[:, :, None, None], (2, RET_HEADS, 1, LANES))
    std_head = np.arange(RET_WIDTH) // HEAD_DIM
    lg_std = decay[:, std_head].reshape(2, N_PAIRS, 1, LANES)
    lg_pair = decay[:, _PAIR8 // HEAD_DIM].reshape(2, N_PAIRS, 1, LANES)

    w_ffn_in_b = w_ffn_in[layer].astype(BF16)
    w_ffn_out_b = w_ffn_out[layer].astype(BF16)
    w_out_b = w_out[layer].astype(BF16)

    c_rows = jnp.concatenate([c, c_ctx[None, :], jnp.zeros((8 - bsz - 1, D_MODEL), c.dtype)], axis=0)
    mod = _mod_call(c_rows, w_mod[layer], b_mod[layer]).reshape(8, N_MOD, D_MODEL)

    g_pre = g_pre_mix[layer].reshape(1, D_MODEL)
    qa, ka_l, va_l, qr, kr, vr, gr = _proj_call(x, mod, g_pre, w_proj, gq_tab, gk_tab, seg_pair,
                                                cos_t, sin_t, latent=True, tm=512)
    ka_c, va_c, kr_c, vr_c = _proj_call(ctx, mod[bsz:bsz + 1], g_pre, w_proj, None, gk_tab, None,
                                        None, None, latent=False, tm=ctx.shape[1])

    tk = va_l.shape[-1]
    att = _attn_call(qa, ka_c, va_c, ka_l.reshape(bsz, n_lat // tk, tk, 4 * KV_WIDTH), va_l, tq=512)
    o_f, o_b = _ret_call(lg_head, lg_std, lg_pair, qr, kr, vr, kr_c, vr_c)

    return _merge_ffn_call(x, att, o_f, o_b, gr, mod,
                           g_post_mix[layer].reshape(1, D_MODEL),
                           g_pre_ffn[layer].reshape(1, D_MODEL),
                           g_post_ffn[layer].reshape(1, D_MODEL),
                           seg_std, w_out_b, w_ffn_in_b, w_ffn_out_b, tm=512)
```

```python
import functools
import math

import numpy as np
import jax
import jax.numpy as jnp
from jax import lax
from jax.experimental import pallas as pl
from jax.experimental.pallas import tpu as pltpu

D_MODEL = 1024
HEAD_DIM = 64
HALF = HEAD_DIM // 2
GRID_W = 64
ATTN_HEADS = 8
ATTN_KV_HEADS = 2
RET_HEADS = 8
ATTN_WIDTH = ATTN_HEADS * HEAD_DIM
KV_WIDTH = ATTN_KV_HEADS * HEAD_DIM
RET_WIDTH = RET_HEADS * HEAD_DIM
FFN_HIDDEN = 2816
FFN_CHUNK = 256
N_MOD = 6
RET_CHUNK = 128
RET_CHUNKS_PER_STEP = 2
ROPE_BASE = 10000.0
ATTN_SCALE = HEAD_DIM ** -0.5
EPS = 1e-6
LOG2E = math.log2(math.e)

LANES = 128
MXU_DEPTH = 256
F32_MAX_EXP = 127.0
ONES_ROWS = 16
V_ROWS = HEAD_DIM + ONES_ROWS
VMEM_LIMIT = 56 * 1024 * 1024

F32 = jnp.float32
BF16 = jnp.bfloat16

_QA0 = 0
_KA0 = ATTN_WIDTH
_VA0 = ATTN_WIDTH + KV_WIDTH
_QR0 = ATTN_WIDTH + 2 * KV_WIDTH
_KR0 = _QR0 + RET_WIDTH
_VR0 = _KR0 + RET_WIDTH
_GR0 = _VR0 + RET_WIDTH


def _pair_perm(n_heads):
    idx = []
    for p in range(n_heads // 2):
        for seg in range(4):
            head = 2 * p + (seg % 2)
            lo = (seg // 2) * HALF
            idx.extend(head * HEAD_DIM + lo + d for d in range(HALF))
    return np.asarray(idx, np.int32)


_PAIR8 = _pair_perm(8)


def _pair_layout(w):
    rows, width = w.shape
    return (w.reshape(rows, width // LANES, 2, 2, HALF).transpose(0, 1, 3, 2, 4)
            .reshape(rows, width))


def _projection_weights(w):
    rows = w.shape[0]
    ka = w[:, _KA0:_KA0 + KV_WIDTH].reshape(rows, ATTN_KV_HEADS, 2, 1, HALF)
    ka = jnp.broadcast_to(ka, (rows, ATTN_KV_HEADS, 2, 2, HALF)).reshape(rows, 2 * KV_WIDTH)
    va = w[:, _VA0:_VA0 + KV_WIDTH]
    groups = [("qa", _pair_layout(w[:, _QA0:_QA0 + ATTN_WIDTH])), ("ka", ka), ("va", va),
              ("qr", _pair_layout(w[:, _QR0:_QR0 + RET_WIDTH])),
              ("kr", _pair_layout(w[:, _KR0:_KR0 + RET_WIDTH])),
              ("vr", w[:, _VR0:_VR0 + RET_WIDTH]), ("gr", w[:, _GR0:_GR0 + RET_WIDTH])]
    return jnp.concatenate([g for _, g in groups], axis=1)


_PROJ_COLS = {"qa": (0, ATTN_WIDTH), "ka": (ATTN_WIDTH, 2 * KV_WIDTH),
              "va": (ATTN_WIDTH + 2 * KV_WIDTH, KV_WIDTH)}
for _i, _name in enumerate(("qr", "kr", "vr", "gr")):
    _PROJ_COLS[_name] = (ATTN_WIDTH + 3 * KV_WIDTH + _i * RET_WIDTH, RET_WIDTH)
_PAIR_DIM = (_PAIR8[:LANES] % HEAD_DIM).astype(np.int32)


def _dot(a, b):
    return jnp.dot(a, b, preferred_element_type=F32)


def _dot_nt(a, b):
    return lax.dot_general(a, b, (((1,), (1,)), ((), ())), preferred_element_type=F32)


def _dot_tn(a, b):
    return lax.dot_general(a, b, (((0,), (0,)), ((), ())), preferred_element_type=F32)


def _head_sums(a, same_head_bf16, exact=False):
    hi = a.astype(BF16)
    out = _dot(hi, same_head_bf16)
    if exact:
        out = out + _dot((a - hi.astype(F32)).astype(BF16), same_head_bf16)
    return out


def _silu(x):
    return x * (1.0 / (1.0 + jnp.exp(-x)))


def _lane_iota(shape):
    return lax.broadcasted_iota(jnp.int32, shape, len(shape) - 1)


def _mod_kernel(c_ref, w_ref, b_ref, o_ref):
    h = _silu(c_ref[...]).astype(BF16)
    o_ref[...] = _dot(h, w_ref[...].astype(BF16)) + b_ref[...]


def _mod_call(c_rows, w_mod, b_mod):
    rows = c_rows.shape[0]
    n_out = w_mod.shape[1]
    tn = D_MODEL
    return pl.pallas_call(
        _mod_kernel,
        out_shape=jax.ShapeDtypeStruct((rows, n_out), F32),
        grid=(n_out // tn,),
        in_specs=[
            pl.BlockSpec((rows, D_MODEL), lambda j: (0, 0)),
            pl.BlockSpec((D_MODEL, tn), lambda j: (0, j)),
            pl.BlockSpec((1, tn), lambda j: (0, j)),
        ],
        out_specs=pl.BlockSpec((rows, tn), lambda j: (0, j)),
        compiler_params=pltpu.CompilerParams(dimension_semantics=("arbitrary",),
                                             vmem_limit_bytes=VMEM_LIMIT),
        name="adaln_mod",
    )(c_rows, w_mod, b_mod.reshape(1, n_out))


def _rope(y, cos_t, sin_t):
    blocks = []
    for p in range(y.shape[1] // LANES):
        yp = y[:, p * LANES:(p + 1) * LANES]
        blocks.append(yp * cos_t + pltpu.roll(yp, LANES // 2, 1) * sin_t)
    return blocks[0] if len(blocks) == 1 else jnp.concatenate(blocks, axis=1)


def _proj_kernel(*refs, latent):
    if latent:
        (x_ref, mod_ref, g_ref, w_ref, gq_ref, gk_ref, seg_ref, cos_ref, sin_ref,
         qa_ref, ka_ref, va_ref, qr_ref, kr_ref, vr_ref, gr_ref) = refs
    else:
        (x_ref, mod_ref, g_ref, w_ref, gk_ref,
         ka_ref, va_ref, kr_ref, vr_ref) = refs

    x = x_ref[0]
    ms = jnp.mean(x * x, axis=-1, keepdims=True)
    shift = mod_ref[0, 0:1, :]
    scale = mod_ref[0, 1:2, :]
    h = (x * lax.rsqrt(ms + EPS) * g_ref[...]) * (1.0 + scale) + shift
    hb = h.astype(BF16)

    def proj(name):
        start, width = _PROJ_COLS[name]
        return _dot(hb, w_ref[:, start:start + width])

    if latent:
        cos_t = cos_ref[...]
        sin_t = sin_ref[...]
        rope = lambda y: _rope(y, cos_t, sin_t)
        yq = proj("qa")
        ssq = _head_sums(yq * yq, seg_ref[...])
        gq = jnp.concatenate([gq_ref[...]] * (ATTN_WIDTH // LANES), axis=1)
        qn = yq * lax.rsqrt(ssq * (1.0 / HEAD_DIM) + EPS) * gq
        qa_ref[0] = (rope(qn) * (ATTN_SCALE * LOG2E)).T.astype(BF16)
    else:
        rope = lambda y: y

    yk = proj("ka")
    slot_b = (_lane_iota((yk.shape[0], LANES)) // HALF) % 2 == 1
    kblocks = []
    for kv in range(ATTN_KV_HEADS):
        ykv = yk[:, kv * LANES:(kv + 1) * LANES]
        ssq = 0.5 * jnp.sum(ykv * ykv, axis=-1, keepdims=True)
        kn = rope(ykv * lax.rsqrt(ssq * (1.0 / HEAD_DIM) + EPS) * gk_ref[...])
        kblocks += [jnp.where(slot_b, 0.0, kn), jnp.where(slot_b, kn, 0.0)]
    ka_ref[0] = jnp.concatenate(kblocks, axis=1).astype(BF16)

    yv_t = proj("va").T
    ones = jnp.ones((ONES_ROWS, yv_t.shape[1]), F32)
    va_ref[0, 0] = jnp.concatenate(
        [blk for kv in range(ATTN_KV_HEADS)
         for blk in (yv_t[kv * HEAD_DIM:(kv + 1) * HEAD_DIM], ones)], axis=0).astype(BF16)

    if latent:
        qr_ref[0] = rope(proj("qr")).astype(BF16)
    kr_ref[0] = rope(proj("kr") * ATTN_SCALE).astype(BF16)
    vr_ref[0] = proj("vr").astype(BF16)
    if latent:
        gr_ref[0] = proj("gr")


def _proj_call(x, mod, g_pre, w, gq_tab, gk_tab, seg, cos_t, sin_t, *, latent, tm):
    bsz, n, _ = x.shape
    wcols = w.shape[1]
    tok = lambda width: pl.BlockSpec((1, tm, width), lambda b, i: (b, i, 0))
    full = lambda shape: pl.BlockSpec(shape, lambda b, i: (0,) * len(shape))
    mod_spec = pl.BlockSpec((1, N_MOD, D_MODEL), (lambda b, i: (b, 0, 0)) if latent
                            else (lambda b, i: (0, 0, 0)))
    if latent:
        in_specs = [tok(D_MODEL), mod_spec, full((1, D_MODEL)), full((D_MODEL, wcols)),
                    full((1, LANES)), full((1, LANES)), full((ATTN_WIDTH, ATTN_WIDTH)),
                    pl.BlockSpec((tm, LANES), lambda b, i: (i, 0)),
                    pl.BlockSpec((tm, LANES), lambda b, i: (i, 0))]
        args = (x, mod, g_pre, w, gq_tab, gk_tab, seg, cos_t, sin_t)
    else:
        in_specs = [tok(D_MODEL), mod_spec, full((1, D_MODEL)), full((D_MODEL, wcols)),
                    full((1, LANES))]
        args = (x, mod, g_pre, w, gk_tab)
    token_major = lambda width, dt: (jax.ShapeDtypeStruct((bsz, n, width), dt), tok(width))
    q_t = (jax.ShapeDtypeStruct((bsz, ATTN_WIDTH, n), BF16),
           pl.BlockSpec((1, ATTN_WIDTH, tm), lambda b, i: (b, 0, i)))
    k_ab = token_major(4 * KV_WIDTH, BF16)
    v_t = (jax.ShapeDtypeStruct((bsz, n // tm, ATTN_KV_HEADS * V_ROWS, tm), BF16),
           pl.BlockSpec((1, 1, ATTN_KV_HEADS * V_ROWS, tm), lambda b, i: (b, i, 0, 0)))
    ret = token_major(RET_WIDTH, BF16)
    outs = ([q_t, k_ab, v_t, ret, ret, ret, token_major(RET_WIDTH, F32)] if latent
            else [k_ab, v_t, ret, ret])
    return pl.pallas_call(
        functools.partial(_proj_kernel, latent=latent),
        out_shape=[o[0] for o in outs],
        grid=(bsz, n // tm),
        in_specs=in_specs,
        out_specs=[o[1] for o in outs],
        compiler_params=pltpu.CompilerParams(dimension_semantics=("arbitrary", "arbitrary"),
                                             vmem_limit_bytes=VMEM_LIMIT),
        name="in_proj_latent" if latent else "in_proj_ctx",
    )(*args)


LOOKAHEAD = 3
TILES_PER_TRIP = 8
ITEM_KEYS = 512
DRAIN_ITEMS = 2
Q_SPLITS = 2


def _attn_kernel(lag_ok_ref, qt_ref, kc_ref, vc_ref, kl_ref, vl_ref, o_ref, m_ref, acc_ref, s_ref):
    n_heads = 2 * (qt_ref.shape[1] // LANES)
    tq = qt_ref.shape[2] // Q_SPLITS
    n_streams = n_heads * Q_SPLITS

    m_ref[...] = jnp.full(m_ref.shape, -jnp.inf, F32)
    acc_ref[...] = jnp.zeros(acc_ref.shape, F32)

    n_tiles = kl_ref.shape[1]

    def scores(k_ref, item):
        sub, stream = divmod(item, n_streams)
        qb, h = divmod(stream, n_heads)
        q_t = qt_ref[0, (h // 2) * LANES:(h // 2 + 1) * LANES, qb * tq:(qb + 1) * tq]
        keys = min(ITEM_KEYS, k_ref.shape[0])
        half = keys // 2
        lo = sub * keys
        cols = slice((h % 2) * LANES, (h % 2 + 1) * LANES)
        return (_dot(k_ref[lo:lo + half, cols], q_t), _dot(k_ref[lo + half:lo + keys, cols], q_t))

    def accumulate(item, s, v_ref, lagged):
        sub, h = divmod(item, n_streams)
        half = s[0].shape[0]
        lo = sub * 2 * half
        m_old = m_ref[h]
        m_new = jnp.maximum(m_old, jnp.maximum(jnp.max(s[0], axis=0, keepdims=True),
                                               jnp.max(s[1], axis=0, keepdims=True)))
        alpha = jnp.exp2(m_old - m_new)
        shift = m_old if lagged else m_new
        p0 = jnp.exp2(s[0] - shift).astype(BF16)
        p1 = jnp.exp2(s[1] - shift).astype(BF16)
        if 2 * half <= MXU_DEPTH:
            pv = _dot(v_ref[:, lo:lo + 2 * half], jnp.concatenate([p0, p1], axis=0))
        else:
            pv = _dot(v_ref[:, lo:lo + half], p0) + _dot(v_ref[:, lo + half:lo + 2 * half], p1)
        acc_ref[h] = alpha * (acc_ref[h] + pv) if lagged else alpha * acc_ref[h] + pv
        m_ref[h] = m_new

    def run_items(s_ahead, tiles, after, lagged):
        per_tile = [n_streams * max(1, k_ref.shape[0] // ITEM_KEYS) for k_ref, _ in tiles]
        refs = [(k_ref, v_ref, i) for (k_ref, v_ref), n in zip(tiles, per_tile) for i in range(n)]
        refs_after = [(after, None, i) for i in range(LOOKAHEAD)]
        total = len(refs)
        requests = [[i + LOOKAHEAD] for i in range(total)]
        drain = min(DRAIN_ITEMS, total // 2)
        for i in range(total - drain, total):
            requests[i - drain] += requests[i]
            requests[i] = []
        pending = dict(enumerate(s_ahead))
        for i, (k_ref, v_ref, item) in enumerate(refs):
            for r in requests[i]:
                rk, _, ritem = (refs + refs_after)[r]
                pending[r] = scores(rk, ritem)
            accumulate(item, pending.pop(i), v_ref, lagged)
        return [pending[total + j] for j in range(LOOKAHEAD)]

    def save(s_ahead):
        for i, s in enumerate(s_ahead):
            s_ref[i, 0] = s[0]
            s_ref[i, 1] = s[1]

    ctx_k = kc_ref.at[0]
    save(run_items([scores(ctx_k, i) for i in range(LOOKAHEAD)], [(ctx_k, vc_ref.at[0, 0])],
                   kl_ref.at[0, 0], lagged=False))

    def latent_tiles(lagged):
        def body(jj, carry):
            j0 = jj * TILES_PER_TRIP
            tiles = [(kl_ref.at[0, j0 + u], vl_ref.at[0, j0 + u]) for u in range(TILES_PER_TRIP)]
            after = kl_ref.at[0, jnp.minimum(j0 + TILES_PER_TRIP, n_tiles - 1)]
            save(run_items([(s_ref[i, 0], s_ref[i, 1]) for i in range(LOOKAHEAD)], tiles, after,
                           lagged))
            return carry

        lax.fori_loop(0, n_tiles // TILES_PER_TRIP, body, 0)

    assert n_tiles % TILES_PER_TRIP == 0
    lag_ok = lag_ok_ref[0] != 0
    pl.when(lag_ok)(lambda: latent_tiles(True))
    pl.when(jnp.logical_not(lag_ok))(lambda: latent_tiles(False))

    for qb in range(Q_SPLITS):
        heads = []
        for h in range(n_heads):
            acc = acc_ref[qb * n_heads + h]
            heads.append(acc[:HEAD_DIM] / acc[HEAD_DIM:HEAD_DIM + 1])
        o_ref[0, qb * tq:(qb + 1) * tq, :] = jnp.concatenate(heads, axis=0).T.astype(o_ref.dtype)


def _attn_call(lag_ok, qa_t, ka_c, va_c, ka_l, va_l, *, tq):
    bsz, _, n = qa_t.shape
    n_ctx = ka_c.shape[1]
    n_tiles, tk = ka_l.shape[1], ka_l.shape[2]
    gw = ATTN_WIDTH // ATTN_KV_HEADS
    n_heads = ATTN_HEADS // ATTN_KV_HEADS
    return pl.pallas_call(
        _attn_kernel,
        out_shape=jax.ShapeDtypeStruct((bsz, n, ATTN_WIDTH), BF16),
        grid=(bsz, ATTN_KV_HEADS, n // tq),
        in_specs=[pl.BlockSpec(memory_space=pltpu.SMEM),
                  pl.BlockSpec((1, gw, tq), lambda b, g, i: (b, g, i)),
                  pl.BlockSpec((1, n_ctx, 2 * LANES), lambda b, g, i: (b, 0, g)),
                  pl.BlockSpec((1, 1, V_ROWS, n_ctx), lambda b, g, i: (b, 0, g, 0)),
                  pl.BlockSpec((1, n_tiles, tk, 2 * LANES), lambda b, g, i: (b, 0, 0, g)),
                  pl.BlockSpec((1, n_tiles, V_ROWS, tk), lambda b, g, i: (b, 0, g, 0))],
        out_specs=pl.BlockSpec((1, tq, gw), lambda b, g, i: (b, i, g)),
        scratch_shapes=[pltpu.VMEM((n_heads * Q_SPLITS, 1, tq // Q_SPLITS), F32),
                        pltpu.VMEM((n_heads * Q_SPLITS, V_ROWS, tq // Q_SPLITS), F32),
                        pltpu.VMEM((LOOKAHEAD, 2, ITEM_KEYS // 2, tq // Q_SPLITS), F32)],
        compiler_params=pltpu.CompilerParams(
            dimension_semantics=("arbitrary", "arbitrary", "arbitrary"),
            vmem_limit_bytes=VMEM_LIMIT),
        name="gqa_flash_attention",
    )(lag_ok, qa_t, ka_c, va_c, ka_l, va_l)


N_PAIRS = RET_HEADS // 2


def _ret_kernel(lg_head_ref, lg_std_ref, lg_pair_ref,
                qm_ref, km_ref, vm_ref, qb_ref, kb_ref, vb_ref, kc_ref, vc_ref,
                om_ref, ob_ref,
                dsum_ref, qdec_ref, kdec_ref, cdec_ref, state_ref):
    C = RET_CHUNK
    n_ctx = kc_ref.shape[1]
    first_call = (pl.program_id(0) == 0) & (pl.program_id(1) == 0)

    std_a = _lane_iota((C, LANES)) < HEAD_DIM
    pair_a = (_lane_iota((C, LANES)) // HALF) % 2 == 0
    row_head = (lax.broadcasted_iota(jnp.int32, (LANES, LANES), 0) // HALF) % 2
    col_head = lax.broadcasted_iota(jnp.int32, (LANES, LANES), 1) // HEAD_DIM
    same_head = row_head == col_head

    @pl.when(first_call)
    def _tables():
        row = lax.broadcasted_iota(jnp.int32, (C, C), 0).astype(F32)
        colv = lax.broadcasted_iota(jnp.int32, (C, C), 1).astype(F32)
        pos = row[:, :LANES]
        for h in range(RET_HEADS):
            lg_f = -jnp.exp(lg_head_ref[0, h])
            lg_b = -jnp.exp(lg_head_ref[1, h])
            dsum_ref[h] = (jnp.where(row >= colv, jnp.exp(lg_f * jnp.maximum(row - colv, 0.0)), 0.0)
                           + jnp.where(colv >= row, jnp.exp(lg_b * jnp.maximum(colv - row, 0.0)), 0.0))
        for d in range(2):
            for p in range(N_PAIRS):
                lg_std = -jnp.exp(lg_std_ref[d, p])
                lg_pair = -jnp.exp(lg_pair_ref[d, p])
                q_steps = (pos + 1.0) if d == 0 else (C - pos)
                k_steps = (C - 1.0 - pos) if d == 0 else pos
                qdec_ref[d, p] = jnp.exp(lg_std * q_steps)
                kdec_ref[d, p] = jnp.exp(lg_pair * k_steps)
                cdec_ref[d, p] = jnp.exp(lg_std * float(C))

    @pl.when(pl.program_id(1) == 0)
    def _seed_states():
        cpos = lax.broadcasted_iota(jnp.int32, (n_ctx, LANES), 0).astype(F32)
        for d in range(2):
            steps = (n_ctx - 1.0 - cpos) if d == 0 else cpos
            for p in range(N_PAIRS):
                lg_pair = -jnp.exp(lg_pair_ref[d, p])
                kc = kc_ref[0, :, p * LANES:(p + 1) * LANES].astype(F32) * jnp.exp(lg_pair * steps)
                vc = vc_ref[0, :, p * LANES:(p + 1) * LANES]
                state_ref[d, p] = jnp.where(same_head, _dot_tn(kc.astype(BF16), vc), 0.0)

    def decayed(x, dec):
        return (x.astype(F32) * dec).astype(BF16)

    for u in range(RET_CHUNKS_PER_STEP):
        rows_m = slice(u * C, (u + 1) * C)
        rows_b = slice((RET_CHUNKS_PER_STEP - 1 - u) * C, (RET_CHUNKS_PER_STEP - u) * C)

        stage1 = []
        for p in range(N_PAIRS):
            sl = slice(p * LANES, (p + 1) * LANES)
            q, k, v = qm_ref[0, rows_m, sl], km_ref[0, rows_m, sl], vm_ref[0, rows_m, sl]
            q2, k2, v2 = qb_ref[0, rows_b, sl], kb_ref[0, rows_b, sl], vb_ref[0, rows_b, sl]
            zk = jnp.zeros_like(k)
            k_split = jnp.concatenate([jnp.where(pair_a, k, zk), jnp.where(pair_a, zk, k)], axis=0)
            s = _dot_nt(q, k_split)
            cross_f = _dot(q, state_ref[0, p].astype(BF16)) * qdec_ref[0, p]
            cross_b = _dot(q2, state_ref[1, p].astype(BF16)) * qdec_ref[1, p]
            kv_f = _dot_tn(decayed(k, kdec_ref[0, p]), v)
            kv_b = _dot_tn(decayed(k2, kdec_ref[1, p]), v2)
            stage1.append((s, cross_f, cross_b, kv_f, kv_b, v))

        for p, (s, cross_f, cross_b, kv_f, kv_b, v) in enumerate(stage1):
            sl = slice(p * LANES, (p + 1) * LANES)
            w = jnp.concatenate([s[:, :C] * dsum_ref[2 * p], s[:, C:] * dsum_ref[2 * p + 1]], axis=1)
            zv = jnp.zeros_like(v)
            v_split = jnp.concatenate([jnp.where(std_a, v, zv), jnp.where(std_a, zv, v)], axis=0)
            om_ref[0, rows_m, sl] = _dot(w.astype(BF16), v_split) + cross_f
            ob_ref[0, rows_b, sl] = cross_b
            state_ref[0, p] = state_ref[0, p] * cdec_ref[0, p] + jnp.where(same_head, kv_f, 0.0)
            state_ref[1, p] = state_ref[1, p] * cdec_ref[1, p] + jnp.where(same_head, kv_b, 0.0)


def _ret_call(lg_head, lg_std, lg_pair, qr, kr, vr, kr_c, vr_c):
    bsz, n, _ = qr.shape
    n_ctx = kr_c.shape[1]
    C = RET_CHUNK
    rows = C * RET_CHUNKS_PER_STEP
    nc = n // rows
    fwd = pl.BlockSpec((1, rows, RET_WIDTH), lambda b, c: (b, c, 0))
    bwd = pl.BlockSpec((1, rows, RET_WIDTH), lambda b, c: (b, nc - 1 - c, 0))
    ctx = pl.BlockSpec((1, n_ctx, RET_WIDTH), lambda b, c: (b, 0, 0))
    full = lambda a: pl.BlockSpec(a.shape, lambda b, c: (0,) * a.ndim)
    return pl.pallas_call(
        _ret_kernel,
        out_shape=[jax.ShapeDtypeStruct((bsz, n, RET_WIDTH), F32)] * 2,
        grid=(bsz, nc),
        in_specs=[full(lg_head), full(lg_std), full(lg_pair), fwd, fwd, fwd, bwd, bwd, bwd, ctx, ctx],
        out_specs=[fwd, bwd],
        scratch_shapes=[pltpu.VMEM((RET_HEADS, C, C), F32),
                        pltpu.VMEM((2, N_PAIRS, C, LANES), F32),
                        pltpu.VMEM((2, N_PAIRS, C, LANES), F32),
                        pltpu.VMEM((2, N_PAIRS, 1, LANES), F32),
                        pltpu.VMEM((2, N_PAIRS, LANES, LANES), F32)],
        compiler_params=pltpu.CompilerParams(dimension_semantics=("arbitrary", "arbitrary"),
                                             vmem_limit_bytes=VMEM_LIMIT),
        name="bidir_retention",
    )(lg_head, lg_std, lg_pair, qr, kr, vr, qr, kr, vr, kr_c, vr_c)


def _merge_ffn_kernel(x_ref, att_ref, of_ref, ob_ref, gr_ref, mod_ref, gpm_ref, gpf_ref, gqf_ref,
                      seg_ref, wo_ref, wi_ref, w2_ref, out_ref):
    x = x_ref[0]
    gate_m = mod_ref[0, 2:3, :]
    shift_f = mod_ref[0, 3:4, :]
    scale_f = mod_ref[0, 4:5, :]
    gate_f = mod_ref[0, 5:6, :]

    ret = of_ref[0] + ob_ref[0]
    seg = seg_ref[...]
    mu = _head_sums(ret, seg, exact=True) * (1.0 / HEAD_DIM)
    cen = ret - mu
    var = _head_sums(cen * cen, seg) * (1.0 / HEAD_DIM)
    ret_n = cen * lax.rsqrt(var + EPS) * _silu(gr_ref[0])

    mix = (_dot(att_ref[0], wo_ref[:ATTN_WIDTH, :])
           + _dot(ret_n.astype(BF16), wo_ref[ATTN_WIDTH:, :]))
    ms = jnp.mean(mix * mix, axis=-1, keepdims=True)
    x1 = x + gate_m * (mix * lax.rsqrt(ms + EPS) * gpm_ref[...])

    ms1 = jnp.mean(x1 * x1, axis=-1, keepdims=True)
    hf = ((x1 * lax.rsqrt(ms1 + EPS) * gpf_ref[...]) * (1.0 + scale_f) + shift_f).astype(BF16)

    acc = jnp.zeros(x.shape, F32)
    for j in range(FFN_HIDDEN // FFN_CHUNK):
        lo = j * FFN_CHUNK
        a = _dot(hf, wi_ref[:, lo:lo + FFN_CHUNK])
        b = _dot(hf, wi_ref[:, FFN_HIDDEN + lo:FFN_HIDDEN + lo + FFN_CHUNK])
        acc = acc + _dot((_silu(a) * b).astype(BF16), w2_ref[lo:lo + FFN_CHUNK, :])
    ms2 = jnp.mean(acc * acc, axis=-1, keepdims=True)
    out_ref[0] = x1 + gate_f * (acc * lax.rsqrt(ms2 + EPS) * gqf_ref[...])


def _merge_ffn_call(x, att, o_f, o_b, gr, mod, g_post_mix, g_pre_ffn, g_post_ffn, seg, w_out,
                    w_ffn_in, w_ffn_out, *, tm):
    bsz, n, _ = x.shape
    tok = lambda width: pl.BlockSpec((1, tm, width), lambda b, i: (b, i, 0))
    full = lambda a: pl.BlockSpec(a.shape, lambda b, i: (0,) * a.ndim,
                                  pipeline_mode=pl.Buffered(1))
    return pl.pallas_call(
        _merge_ffn_kernel,
        out_shape=jax.ShapeDtypeStruct(x.shape, F32),
        grid=(bsz, n // tm),
        in_specs=[tok(D_MODEL), tok(ATTN_WIDTH), tok(RET_WIDTH), tok(RET_WIDTH), tok(RET_WIDTH),
                  pl.BlockSpec((1, N_MOD, D_MODEL), lambda b, i: (b, 0, 0)),
                  full(g_post_mix), full(g_pre_ffn), full(g_post_ffn), full(seg),
                  full(w_out), full(w_ffn_in), full(w_ffn_out)],
        out_specs=tok(D_MODEL),
        compiler_params=pltpu.CompilerParams(dimension_semantics=("arbitrary", "arbitrary"),
                                             vmem_limit_bytes=VMEM_LIMIT),
        name="merge_outproj_ffn",
    )(x, att, o_f, o_b, gr, mod, g_post_mix, g_pre_ffn, g_post_ffn, seg, w_out, w_ffn_in, w_ffn_out)


def _rope_tables(n_lat):
    rows = n_lat // GRID_W
    n_freq = HEAD_DIM // 4
    inv = ROPE_BASE ** (-jnp.arange(n_freq, dtype=F32) / n_freq)
    ang_row = jnp.arange(rows, dtype=F32)[:, None] * inv
    ang_col = jnp.arange(GRID_W, dtype=F32)[:, None] * inv

    def table(fn, signed):
        per_token = jnp.concatenate(
            [jnp.broadcast_to(fn(ang_row)[:, None, :], (rows, GRID_W, n_freq)),
             jnp.broadcast_to(fn(ang_col)[None, :, :], (rows, GRID_W, n_freq))], axis=-1)
        per_token = per_token.reshape(n_lat, HALF)
        halves = [-per_token, -per_token, per_token, per_token] if signed else [per_token] * 4
        return jnp.concatenate(halves, axis=-1)

    return table(jnp.cos, False), table(jnp.sin, True)


def _lagged_softmax_is_safe(gq, gk, v_ctx, v_lat):
    rounding = 1.03
    bound = (HEAD_DIM * ATTN_SCALE * LOG2E * rounding) * jnp.max(jnp.abs(gq)) * jnp.max(jnp.abs(gk))
    v_max = jnp.maximum(jnp.max(jnp.abs(v_ctx)), jnp.max(jnp.abs(v_lat))).astype(F32)
    n_keys = v_ctx.shape[-1] + v_lat.shape[1] * v_lat.shape[-1]
    log2_numerator = 2.0 * bound + math.log2(n_keys) + jnp.log2(jnp.maximum(v_max, 1.0))
    return (log2_numerator <= F32_MAX_EXP - 2.0).astype(jnp.int32).reshape(1)


def _segment_matrix(width, same):
    i = np.arange(width)
    return jnp.asarray(same(i[:, None], i[None, :]), BF16)


def kernel(x, c, ctx, c_ctx, w_mod, b_mod, g_pre_mix, g_post_mix, g_pre_ffn, g_post_ffn,
           w_in, q_norm_g, k_norm_g, ret_decay_fwd, ret_decay_bwd, w_out, w_ffn_in, w_ffn_out):
    bsz, n_lat, _ = x.shape
    assert w_mod.shape[0] == 1, "single-layer block"
    layer = 0

    w_proj = _projection_weights(w_in[layer]).astype(BF16)
    gq_tab = q_norm_g[layer][_PAIR_DIM].reshape(1, LANES)
    gk_tab = k_norm_g[layer][_PAIR_DIM].reshape(1, LANES)
    seg_pair = _segment_matrix(ATTN_WIDTH, lambda i, j: (i // LANES == j // LANES)
                               & ((i // HALF) % 2 == (j // HALF) % 2))
    seg_std = _segment_matrix(RET_WIDTH, lambda i, j: i // HEAD_DIM == j // HEAD_DIM)
    cos_t, sin_t = _rope_tables(n_lat)

    decay = jnp.stack([ret_decay_fwd[layer], ret_decay_bwd[layer]]).astype(F32)
    lg_head = jnp.broadcast_to(decay[:, :, None, None], (2, RET_HEADS, 1, LANES))
    std_head = np.arange(RET_WIDTH) // HEAD_DIM
    lg_std = decay[:, std_head].reshape(2, N_PAIRS, 1, LANES)
    lg_pair = decay[:, _PAIR8 // HEAD_DIM].reshape(2, N_PAIRS, 1, LANES)

    w_ffn_in_b = w_ffn_in[layer].astype(BF16)
    w_ffn_out_b = w_ffn_out[layer].astype(BF16)
    w_out_b = w_out[layer].astype(BF16)

    c_rows = jnp.concatenate([c, c_ctx[None, :], jnp.zeros((8 - bsz - 1, D_MODEL), c.dtype)], axis=0)
    mod = _mod_call(c_rows, w_mod[layer], b_mod[layer]).reshape(8, N_MOD, D_MODEL)

    g_pre = g_pre_mix[layer].reshape(1, D_MODEL)
    qa, ka_l, va_l, qr, kr, vr, gr = _proj_call(x, mod, g_pre, w_proj, gq_tab, gk_tab, seg_pair,
                                                cos_t, sin_t, latent=True, tm=512)
    ka_c, va_c, kr_c, vr_c = _proj_call(ctx, mod[bsz:bsz + 1], g_pre, w_proj, None, gk_tab, None,
                                        None, None, latent=False, tm=ctx.shape[1])

    tk = va_l.shape[-1]
    lag_ok = _lagged_softmax_is_safe(q_norm_g[layer], k_norm_g[layer], va_c, va_l)
    att = _attn_call(lag_ok, qa, ka_c, va_c, ka_l.reshape(bsz, n_lat // tk, tk, 4 * KV_WIDTH), va_l,
                     tq=512)
    o_f, o_b = _ret_call(lg_head, lg_std, lg_pair, qr, kr, vr, kr_c, vr_c)

    return _merge_ffn_call(x, att, o_f, o_b, gr, mod,
                           g_post_mix[layer].reshape(1, D_MODEL),
                           g_pre_ffn[layer].reshape(1, D_MODEL),
                           g_post_ffn[layer].reshape(1, D_MODEL),
                           seg_std, w_out_b, w_ffn_in_b, w_ffn_out_b, tm=512)
```

```python
import functools
import math

import numpy as np
import jax
import jax.numpy as jnp
from jax import lax
from jax.experimental import pallas as pl
from jax.experimental.pallas import tpu as pltpu

D_MODEL = 1024
HEAD_DIM = 64
HALF = HEAD_DIM // 2
GRID_W = 64
ATTN_HEADS = 8
ATTN_KV_HEADS = 2
RET_HEADS = 8
ATTN_WIDTH = ATTN_HEADS * HEAD_DIM
KV_WIDTH = ATTN_KV_HEADS * HEAD_DIM
RET_WIDTH = RET_HEADS * HEAD_DIM
FFN_HIDDEN = 2816
FFN_CHUNK = 256
N_MOD = 6
RET_CHUNK = 128
RET_CHUNKS_PER_STEP = 4
ROPE_BASE = 10000.0
ATTN_SCALE = HEAD_DIM ** -0.5
EPS = 1e-6
LOG2E = math.log2(math.e)

LANES = 128
MXU_DEPTH = 256
F32_MAX_EXP = 127.0
ONES_ROWS = 16
V_ROWS = HEAD_DIM + ONES_ROWS
VMEM_LIMIT = 56 * 1024 * 1024

F32 = jnp.float32
BF16 = jnp.bfloat16

_QA0 = 0
_KA0 = ATTN_WIDTH
_VA0 = ATTN_WIDTH + KV_WIDTH
_QR0 = ATTN_WIDTH + 2 * KV_WIDTH
_KR0 = _QR0 + RET_WIDTH
_VR0 = _KR0 + RET_WIDTH
_GR0 = _VR0 + RET_WIDTH


def _pair_perm(n_heads):
    idx = []
    for p in range(n_heads // 2):
        for seg in range(4):
            head = 2 * p + (seg % 2)
            lo = (seg // 2) * HALF
            idx.extend(head * HEAD_DIM + lo + d for d in range(HALF))
    return np.asarray(idx, np.int32)


_PAIR8 = _pair_perm(8)


def _pair_layout(w):
    rows, width = w.shape
    return (w.reshape(rows, width // LANES, 2, 2, HALF).transpose(0, 1, 3, 2, 4)
            .reshape(rows, width))


def _projection_weights(w):
    rows = w.shape[0]
    ka = w[:, _KA0:_KA0 + KV_WIDTH].reshape(rows, ATTN_KV_HEADS, 2, 1, HALF)
    ka = jnp.broadcast_to(ka, (rows, ATTN_KV_HEADS, 2, 2, HALF)).reshape(rows, 2 * KV_WIDTH)
    va = w[:, _VA0:_VA0 + KV_WIDTH]
    groups = [("qa", _pair_layout(w[:, _QA0:_QA0 + ATTN_WIDTH])), ("ka", ka), ("va", va),
              ("qr", _pair_layout(w[:, _QR0:_QR0 + RET_WIDTH])),
              ("kr", _pair_layout(w[:, _KR0:_KR0 + RET_WIDTH])),
              ("vr", w[:, _VR0:_VR0 + RET_WIDTH]), ("gr", w[:, _GR0:_GR0 + RET_WIDTH])]
    return jnp.concatenate([g for _, g in groups], axis=1)


_PROJ_COLS = {"qa": (0, ATTN_WIDTH), "ka": (ATTN_WIDTH, 2 * KV_WIDTH),
              "va": (ATTN_WIDTH + 2 * KV_WIDTH, KV_WIDTH)}
for _i, _name in enumerate(("qr", "kr", "vr", "gr")):
    _PROJ_COLS[_name] = (ATTN_WIDTH + 3 * KV_WIDTH + _i * RET_WIDTH, RET_WIDTH)
_PAIR_DIM = (_PAIR8[:LANES] % HEAD_DIM).astype(np.int32)


def _dot(a, b):
    return jnp.dot(a, b, preferred_element_type=F32)


def _dot_nt(a, b):
    return lax.dot_general(a, b, (((1,), (1,)), ((), ())), preferred_element_type=F32)


def _dot_tn(a, b):
    return lax.dot_general(a, b, (((0,), (0,)), ((), ())), preferred_element_type=F32)


def _head_sums(a, same_head_bf16, exact=False):
    hi = a.astype(BF16)
    out = _dot(hi, same_head_bf16)
    if exact:
        out = out + _dot((a - hi.astype(F32)).astype(BF16), same_head_bf16)
    return out


def _silu(x):
    return x * (1.0 / (1.0 + jnp.exp(-x)))


def _lane_iota(shape):
    return lax.broadcasted_iota(jnp.int32, shape, len(shape) - 1)


def _mod_kernel(c_ref, w_ref, b_ref, o_ref):
    h = _silu(c_ref[...]).astype(BF16)
    o_ref[...] = _dot(h, w_ref[...].astype(BF16)) + b_ref[...]


def _mod_call(c_rows, w_mod, b_mod):
    rows = c_rows.shape[0]
    n_out = w_mod.shape[1]
    tn = D_MODEL
    return pl.pallas_call(
        _mod_kernel,
        out_shape=jax.ShapeDtypeStruct((rows, n_out), F32),
        grid=(n_out // tn,),
        in_specs=[
            pl.BlockSpec((rows, D_MODEL), lambda j: (0, 0)),
            pl.BlockSpec((D_MODEL, tn), lambda j: (0, j)),
            pl.BlockSpec((1, tn), lambda j: (0, j)),
        ],
        out_specs=pl.BlockSpec((rows, tn), lambda j: (0, j)),
        compiler_params=pltpu.CompilerParams(dimension_semantics=("arbitrary",),
                                             vmem_limit_bytes=VMEM_LIMIT),
        name="adaln_mod",
    )(c_rows, w_mod, b_mod.reshape(1, n_out))


def _rope(y, cos_t, sin_t):
    blocks = []
    for p in range(y.shape[1] // LANES):
        yp = y[:, p * LANES:(p + 1) * LANES]
        blocks.append(yp * cos_t + pltpu.roll(yp, LANES // 2, 1) * sin_t)
    return blocks[0] if len(blocks) == 1 else jnp.concatenate(blocks, axis=1)


def _proj_kernel(*refs, latent):
    if latent:
        (x_ref, mod_ref, g_ref, w_ref, gq_ref, gk_ref, seg_ref, cos_ref, sin_ref,
         qa_ref, ka_ref, va_ref, qr_ref, kr_ref, vr_ref, gr_ref) = refs
    else:
        (x_ref, mod_ref, g_ref, w_ref, gk_ref,
         ka_ref, va_ref, kr_ref, vr_ref) = refs

    x = x_ref[0]
    ms = jnp.mean(x * x, axis=-1, keepdims=True)
    shift = mod_ref[0, 0:1, :]
    scale = mod_ref[0, 1:2, :]
    h = (x * lax.rsqrt(ms + EPS) * g_ref[...]) * (1.0 + scale) + shift
    hb = h.astype(BF16)

    def proj(name):
        start, width = _PROJ_COLS[name]
        return _dot(hb, w_ref[:, start:start + width])

    if latent:
        cos_t = cos_ref[...]
        sin_t = sin_ref[...]
        rope = lambda y: _rope(y, cos_t, sin_t)
        yq = proj("qa")
        ssq = _head_sums(yq * yq, seg_ref[...])
        gq = jnp.concatenate([gq_ref[...]] * (ATTN_WIDTH // LANES), axis=1)
        qn = yq * lax.rsqrt(ssq * (1.0 / HEAD_DIM) + EPS) * gq
        qa_ref[0] = (rope(qn) * (ATTN_SCALE * LOG2E)).T.astype(BF16)
    else:
        rope = lambda y: y

    yk = proj("ka")
    slot_b = (_lane_iota((yk.shape[0], LANES)) // HALF) % 2 == 1
    kblocks = []
    for kv in range(ATTN_KV_HEADS):
        ykv = yk[:, kv * LANES:(kv + 1) * LANES]
        ssq = 0.5 * jnp.sum(ykv * ykv, axis=-1, keepdims=True)
        kn = rope(ykv * lax.rsqrt(ssq * (1.0 / HEAD_DIM) + EPS) * gk_ref[...])
        kblocks += [jnp.where(slot_b, 0.0, kn), jnp.where(slot_b, kn, 0.0)]
    ka_ref[0] = jnp.concatenate(kblocks, axis=1).astype(BF16)

    yv_t = proj("va").T
    ones = jnp.ones((ONES_ROWS, yv_t.shape[1]), F32)
    va_ref[0, 0] = jnp.concatenate(
        [blk for kv in range(ATTN_KV_HEADS)
         for blk in (yv_t[kv * HEAD_DIM:(kv + 1) * HEAD_DIM], ones)], axis=0).astype(BF16)

    if latent:
        qr_ref[0] = rope(proj("qr")).astype(BF16)
    kr_ref[0] = rope(proj("kr") * ATTN_SCALE).astype(BF16)
    vr_ref[0] = proj("vr").astype(BF16)
    if latent:
        gr_ref[0] = proj("gr")


def _proj_call(x, mod, g_pre, w, gq_tab, gk_tab, seg, cos_t, sin_t, *, latent, tm):
    bsz, n, _ = x.shape
    wcols = w.shape[1]
    tok = lambda width: pl.BlockSpec((1, tm, width), lambda b, i: (b, i, 0))
    full = lambda shape: pl.BlockSpec(shape, lambda b, i: (0,) * len(shape))
    mod_spec = pl.BlockSpec((1, N_MOD, D_MODEL), (lambda b, i: (b, 0, 0)) if latent
                            else (lambda b, i: (0, 0, 0)))
    if latent:
        in_specs = [tok(D_MODEL), mod_spec, full((1, D_MODEL)), full((D_MODEL, wcols)),
                    full((1, LANES)), full((1, LANES)), full((ATTN_WIDTH, ATTN_WIDTH)),
                    pl.BlockSpec((tm, LANES), lambda b, i: (i, 0)),
                    pl.BlockSpec((tm, LANES), lambda b, i: (i, 0))]
        args = (x, mod, g_pre, w, gq_tab, gk_tab, seg, cos_t, sin_t)
    else:
        in_specs = [tok(D_MODEL), mod_spec, full((1, D_MODEL)), full((D_MODEL, wcols)),
                    full((1, LANES))]
        args = (x, mod, g_pre, w, gk_tab)
    token_major = lambda width, dt: (jax.ShapeDtypeStruct((bsz, n, width), dt), tok(width))
    q_t = (jax.ShapeDtypeStruct((bsz, ATTN_WIDTH, n), BF16),
           pl.BlockSpec((1, ATTN_WIDTH, tm), lambda b, i: (b, 0, i)))
    k_ab = token_major(4 * KV_WIDTH, BF16)
    v_t = (jax.ShapeDtypeStruct((bsz, n // tm, ATTN_KV_HEADS * V_ROWS, tm), BF16),
           pl.BlockSpec((1, 1, ATTN_KV_HEADS * V_ROWS, tm), lambda b, i: (b, i, 0, 0)))
    ret = token_major(RET_WIDTH, BF16)
    outs = ([q_t, k_ab, v_t, ret, ret, ret, token_major(RET_WIDTH, F32)] if latent
            else [k_ab, v_t, ret, ret])
    return pl.pallas_call(
        functools.partial(_proj_kernel, latent=latent),
        out_shape=[o[0] for o in outs],
        grid=(bsz, n // tm),
        in_specs=in_specs,
        out_specs=[o[1] for o in outs],
        compiler_params=pltpu.CompilerParams(dimension_semantics=("arbitrary", "arbitrary"),
                                             vmem_limit_bytes=VMEM_LIMIT),
        name="in_proj_latent" if latent else "in_proj_ctx",
    )(*args)


LOOKAHEAD = 2
TILES_PER_TRIP = 8
ITEM_KEYS = 512
DRAIN_ITEMS = 2
Q_SPLITS = 2


def _attn_kernel(lag_ok_ref, qt_ref, kc_ref, vc_ref, kl_ref, vl_ref, o_ref, m_ref, acc_ref, s_ref):
    n_heads = 2 * (qt_ref.shape[1] // LANES)
    tq = qt_ref.shape[2] // Q_SPLITS
    n_streams = n_heads * Q_SPLITS

    m_ref[...] = jnp.full(m_ref.shape, -jnp.inf, F32)
    acc_ref[...] = jnp.zeros(acc_ref.shape, F32)

    n_tiles = kl_ref.shape[1]

    def scores(k_ref, item):
        sub, stream = divmod(item, n_streams)
        qb, h = divmod(stream, n_heads)
        q_t = qt_ref[0, (h // 2) * LANES:(h // 2 + 1) * LANES, qb * tq:(qb + 1) * tq]
        keys = min(ITEM_KEYS, k_ref.shape[0])
        half = keys // 2
        lo = sub * keys
        cols = slice((h % 2) * LANES, (h % 2 + 1) * LANES)
        return (_dot(k_ref[lo:lo + half, cols], q_t), _dot(k_ref[lo + half:lo + keys, cols], q_t))

    def accumulate(item, s, v_ref, lagged):
        sub, h = divmod(item, n_streams)
        half = s[0].shape[0]
        lo = sub * 2 * half
        m_old = m_ref[h]
        m_new = jnp.maximum(m_old, jnp.maximum(jnp.max(s[0], axis=0, keepdims=True),
                                               jnp.max(s[1], axis=0, keepdims=True)))
        alpha = jnp.exp2(m_old - m_new)
        shift = m_old if lagged else m_new
        p0 = jnp.exp2(s[0] - shift).astype(BF16)
        p1 = jnp.exp2(s[1] - shift).astype(BF16)
        if 2 * half <= MXU_DEPTH:
            pv = _dot(v_ref[:, lo:lo + 2 * half], jnp.concatenate([p0, p1], axis=0))
        else:
            pv = _dot(v_ref[:, lo:lo + half], p0) + _dot(v_ref[:, lo + half:lo + 2 * half], p1)
        acc_ref[h] = alpha * (acc_ref[h] + pv) if lagged else alpha * acc_ref[h] + pv
        m_ref[h] = m_new

    def run_items(s_ahead, tiles, after, lagged):
        per_tile = [n_streams * max(1, k_ref.shape[0] // ITEM_KEYS) for k_ref, _ in tiles]
        refs = [(k_ref, v_ref, i) for (k_ref, v_ref), n in zip(tiles, per_tile) for i in range(n)]
        refs_after = [(after, None, i) for i in range(LOOKAHEAD)]
        total = len(refs)
        requests = [[i + LOOKAHEAD] for i in range(total)]
        drain = min(DRAIN_ITEMS, total // 2)
        for i in range(total - drain, total):
            requests[i - drain] += requests[i]
            requests[i] = []
        pending = dict(enumerate(s_ahead))
        for i, (k_ref, v_ref, item) in enumerate(refs):
            for r in requests[i]:
                rk, _, ritem = (refs + refs_after)[r]
                pending[r] = scores(rk, ritem)
            accumulate(item, pending.pop(i), v_ref, lagged)
        return [pending[total + j] for j in range(LOOKAHEAD)]

    def save(s_ahead):
        for i, s in enumerate(s_ahead):
            s_ref[i, 0] = s[0]
            s_ref[i, 1] = s[1]

    ctx_k = kc_ref.at[0]
    save(run_items([scores(ctx_k, i) for i in range(LOOKAHEAD)], [(ctx_k, vc_ref.at[0, 0])],
                   kl_ref.at[0, 0], lagged=False))

    def latent_tiles(lagged):
        def body(jj, carry):
            j0 = jj * TILES_PER_TRIP
            tiles = [(kl_ref.at[0, j0 + u], vl_ref.at[0, j0 + u]) for u in range(TILES_PER_TRIP)]
            after = kl_ref.at[0, jnp.minimum(j0 + TILES_PER_TRIP, n_tiles - 1)]
            save(run_items([(s_ref[i, 0], s_ref[i, 1]) for i in range(LOOKAHEAD)], tiles, after,
                           lagged))
            return carry

        lax.fori_loop(0, n_tiles // TILES_PER_TRIP, body, 0)

    assert n_tiles % TILES_PER_TRIP == 0
    lag_ok = lag_ok_ref[0] != 0
    pl.when(lag_ok)(lambda: latent_tiles(True))
    pl.when(jnp.logical_not(lag_ok))(lambda: latent_tiles(False))

    for qb in range(Q_SPLITS):
        heads = []
        for h in range(n_heads):
            acc = acc_ref[qb * n_heads + h]
            heads.append(acc[:HEAD_DIM] / acc[HEAD_DIM:HEAD_DIM + 1])
        o_ref[0, qb * tq:(qb + 1) * tq, :] = jnp.concatenate(heads, axis=0).T.astype(o_ref.dtype)


def _attn_call(lag_ok, qa_t, ka_c, va_c, ka_l, va_l, *, tq):
    bsz, _, n = qa_t.shape
    n_ctx = ka_c.shape[1]
    n_tiles, tk = ka_l.shape[1], ka_l.shape[2]
    gw = ATTN_WIDTH // ATTN_KV_HEADS
    n_heads = ATTN_HEADS // ATTN_KV_HEADS
    return pl.pallas_call(
        _attn_kernel,
        out_shape=jax.ShapeDtypeStruct((bsz, n, ATTN_WIDTH), BF16),
        grid=(bsz, ATTN_KV_HEADS, n // tq),
        in_specs=[pl.BlockSpec(memory_space=pltpu.SMEM),
                  pl.BlockSpec((1, gw, tq), lambda b, g, i: (b, g, i)),
                  pl.BlockSpec((1, n_ctx, 2 * LANES), lambda b, g, i: (b, 0, g)),
                  pl.BlockSpec((1, 1, V_ROWS, n_ctx), lambda b, g, i: (b, 0, g, 0)),
                  pl.BlockSpec((1, n_tiles, tk, 2 * LANES), lambda b, g, i: (b, 0, 0, g)),
                  pl.BlockSpec((1, n_tiles, V_ROWS, tk), lambda b, g, i: (b, 0, g, 0))],
        out_specs=pl.BlockSpec((1, tq, gw), lambda b, g, i: (b, i, g)),
        scratch_shapes=[pltpu.VMEM((n_heads * Q_SPLITS, 1, tq // Q_SPLITS), F32),
                        pltpu.VMEM((n_heads * Q_SPLITS, V_ROWS, tq // Q_SPLITS), F32),
                        pltpu.VMEM((LOOKAHEAD, 2, ITEM_KEYS // 2, tq // Q_SPLITS), F32)],
        compiler_params=pltpu.CompilerParams(
            dimension_semantics=("arbitrary", "arbitrary", "arbitrary"),
            vmem_limit_bytes=VMEM_LIMIT),
        name="gqa_flash_attention",
    )(lag_ok, qa_t, ka_c, va_c, ka_l, va_l)


N_PAIRS = RET_HEADS // 2


def _ret_kernel(lg_head_ref, lg_std_ref, lg_pair_ref,
                qm_ref, km_ref, vm_ref, qb_ref, kb_ref, vb_ref, kc_ref, vc_ref,
                om_ref, ob_ref,
                dsum_ref, qdec_ref, kdec_ref, cdec_ref, state_ref):
    C = RET_CHUNK
    n_ctx = kc_ref.shape[1]
    first_call = (pl.program_id(0) == 0) & (pl.program_id(1) == 0)

    std_a = _lane_iota((C, LANES)) < HEAD_DIM
    pair_a = (_lane_iota((C, LANES)) // HALF) % 2 == 0
    row_head = (lax.broadcasted_iota(jnp.int32, (LANES, LANES), 0) // HALF) % 2
    col_head = lax.broadcasted_iota(jnp.int32, (LANES, LANES), 1) // HEAD_DIM
    same_head = row_head == col_head

    @pl.when(first_call)
    def _tables():
        row = lax.broadcasted_iota(jnp.int32, (C, C), 0).astype(F32)
        colv = lax.broadcasted_iota(jnp.int32, (C, C), 1).astype(F32)
        pos = row[:, :LANES]
        for h in range(RET_HEADS):
            lg_f = -jnp.exp(lg_head_ref[0, h])
            lg_b = -jnp.exp(lg_head_ref[1, h])
            dsum_ref[h] = (jnp.where(row >= colv, jnp.exp(lg_f * jnp.maximum(row - colv, 0.0)), 0.0)
                           + jnp.where(colv >= row, jnp.exp(lg_b * jnp.maximum(colv - row, 0.0)), 0.0))
        for d in range(2):
            for p in range(N_PAIRS):
                lg_std = -jnp.exp(lg_std_ref[d, p])
                lg_pair = -jnp.exp(lg_pair_ref[d, p])
                q_steps = (pos + 1.0) if d == 0 else (C - pos)
                k_steps = (C - 1.0 - pos) if d == 0 else pos
                qdec_ref[d, p] = jnp.exp(lg_std * q_steps)
                kdec_ref[d, p] = jnp.exp(lg_pair * k_steps)
                cdec_ref[d, p] = jnp.exp(lg_std * float(C))

    @pl.when(pl.program_id(1) == 0)
    def _seed_states():
        cpos = lax.broadcasted_iota(jnp.int32, (n_ctx, LANES), 0).astype(F32)
        for d in range(2):
            steps = (n_ctx - 1.0 - cpos) if d == 0 else cpos
            for p in range(N_PAIRS):
                lg_pair = -jnp.exp(lg_pair_ref[d, p])
                kc = kc_ref[0, :, p * LANES:(p + 1) * LANES].astype(F32) * jnp.exp(lg_pair * steps)
                vc = vc_ref[0, :, p * LANES:(p + 1) * LANES]
                state_ref[d, p] = jnp.where(same_head, _dot_tn(kc.astype(BF16), vc), 0.0)

    def decayed(x, dec):
        return (x.astype(F32) * dec).astype(BF16)

    for u in range(RET_CHUNKS_PER_STEP):
        rows_m = slice(u * C, (u + 1) * C)
        rows_b = slice((RET_CHUNKS_PER_STEP - 1 - u) * C, (RET_CHUNKS_PER_STEP - u) * C)

        stage1 = []
        for p in range(N_PAIRS):
            sl = slice(p * LANES, (p + 1) * LANES)
            q, k, v = qm_ref[0, rows_m, sl], km_ref[0, rows_m, sl], vm_ref[0, rows_m, sl]
            q2, k2, v2 = qb_ref[0, rows_b, sl], kb_ref[0, rows_b, sl], vb_ref[0, rows_b, sl]
            zk = jnp.zeros_like(k)
            k_split = jnp.concatenate([jnp.where(pair_a, k, zk), jnp.where(pair_a, zk, k)], axis=0)
            s = _dot_nt(q, k_split)
            cross_f = _dot(q, state_ref[0, p].astype(BF16)) * qdec_ref[0, p]
            cross_b = _dot(q2, state_ref[1, p].astype(BF16)) * qdec_ref[1, p]
            kv_f = _dot_tn(decayed(k, kdec_ref[0, p]), v)
            kv_b = _dot_tn(decayed(k2, kdec_ref[1, p]), v2)
            stage1.append((s, cross_f, cross_b, kv_f, kv_b, v))

        for p, (s, cross_f, cross_b, kv_f, kv_b, v) in enumerate(stage1):
            sl = slice(p * LANES, (p + 1) * LANES)
            w = jnp.concatenate([s[:, :C] * dsum_ref[2 * p], s[:, C:] * dsum_ref[2 * p + 1]], axis=1)
            zv = jnp.zeros_like(v)
            v_split = jnp.concatenate([jnp.where(std_a, v, zv), jnp.where(std_a, zv, v)], axis=0)
            om_ref[0, rows_m, sl] = (_dot(w.astype(BF16), v_split) + cross_f).astype(om_ref.dtype)
            ob_ref[0, rows_b, sl] = cross_b.astype(ob_ref.dtype)
            state_ref[0, p] = state_ref[0, p] * cdec_ref[0, p] + jnp.where(same_head, kv_f, 0.0)
            state_ref[1, p] = state_ref[1, p] * cdec_ref[1, p] + jnp.where(same_head, kv_b, 0.0)


def _ret_call(lg_head, lg_std, lg_pair, qr, kr, vr, kr_c, vr_c):
    bsz, n, _ = qr.shape
    n_ctx = kr_c.shape[1]
    C = RET_CHUNK
    rows = C * RET_CHUNKS_PER_STEP
    nc = n // rows
    fwd = pl.BlockSpec((1, rows, RET_WIDTH), lambda b, c: (b, c, 0))
    bwd = pl.BlockSpec((1, rows, RET_WIDTH), lambda b, c: (b, nc - 1 - c, 0))
    ctx = pl.BlockSpec((1, n_ctx, RET_WIDTH), lambda b, c: (b, 0, 0))
    full = lambda a: pl.BlockSpec(a.shape, lambda b, c: (0,) * a.ndim)
    return pl.pallas_call(
        _ret_kernel,
        out_shape=[jax.ShapeDtypeStruct((bsz, n, RET_WIDTH), BF16)] * 2,
        grid=(bsz, nc),
        in_specs=[full(lg_head), full(lg_std), full(lg_pair), fwd, fwd, fwd, bwd, bwd, bwd, ctx, ctx],
        out_specs=[fwd, bwd],
        scratch_shapes=[pltpu.VMEM((RET_HEADS, C, C), F32),
                        pltpu.VMEM((2, N_PAIRS, C, LANES), F32),
                        pltpu.VMEM((2, N_PAIRS, C, LANES), F32),
                        pltpu.VMEM((2, N_PAIRS, 1, LANES), F32),
                        pltpu.VMEM((2, N_PAIRS, LANES, LANES), F32)],
        compiler_params=pltpu.CompilerParams(dimension_semantics=("arbitrary", "arbitrary"),
                                             vmem_limit_bytes=VMEM_LIMIT),
        name="bidir_retention",
    )(lg_head, lg_std, lg_pair, qr, kr, vr, qr, kr, vr, kr_c, vr_c)


def _merge_ffn_kernel(x_ref, att_ref, of_ref, ob_ref, gr_ref, mod_ref, gpm_ref, gpf_ref, gqf_ref,
                      seg_ref, wo_ref, wi_ref, w2_ref, out_ref):
    x = x_ref[0]
    gate_m = mod_ref[0, 2:3, :]
    shift_f = mod_ref[0, 3:4, :]
    scale_f = mod_ref[0, 4:5, :]
    gate_f = mod_ref[0, 5:6, :]

    ret = of_ref[0].astype(F32) + ob_ref[0].astype(F32)
    seg = seg_ref[...]
    mu = _head_sums(ret, seg, exact=True) * (1.0 / HEAD_DIM)
    cen = ret - mu
    var = _head_sums(cen * cen, seg) * (1.0 / HEAD_DIM)
    ret_n = cen * lax.rsqrt(var + EPS) * _silu(gr_ref[0])

    mix = (_dot(att_ref[0], wo_ref[:ATTN_WIDTH, :])
           + _dot(ret_n.astype(BF16), wo_ref[ATTN_WIDTH:, :]))
    ms = jnp.mean(mix * mix, axis=-1, keepdims=True)
    x1 = x + gate_m * (mix * lax.rsqrt(ms + EPS) * gpm_ref[...])

    ms1 = jnp.mean(x1 * x1, axis=-1, keepdims=True)
    hf = ((x1 * lax.rsqrt(ms1 + EPS) * gpf_ref[...]) * (1.0 + scale_f) + shift_f).astype(BF16)

    acc = jnp.zeros(x.shape, F32)
    for j in range(FFN_HIDDEN // FFN_CHUNK):
        lo = j * FFN_CHUNK
        a = _dot(hf, wi_ref[:, lo:lo + FFN_CHUNK])
        b = _dot(hf, wi_ref[:, FFN_HIDDEN + lo:FFN_HIDDEN + lo + FFN_CHUNK])
        acc = acc + _dot((_silu(a) * b).astype(BF16), w2_ref[lo:lo + FFN_CHUNK, :])
    ms2 = jnp.mean(acc * acc, axis=-1, keepdims=True)
    out_ref[0] = x1 + gate_f * (acc * lax.rsqrt(ms2 + EPS) * gqf_ref[...])


def _merge_ffn_call(x, att, o_f, o_b, gr, mod, g_post_mix, g_pre_ffn, g_post_ffn, seg, w_out,
                    w_ffn_in, w_ffn_out, *, tm):
    bsz, n, _ = x.shape
    tok = lambda width: pl.BlockSpec((1, tm, width), lambda b, i: (b, i, 0))
    full = lambda a: pl.BlockSpec(a.shape, lambda b, i: (0,) * a.ndim,
                                  pipeline_mode=pl.Buffered(1))
    return pl.pallas_call(
        _merge_ffn_kernel,
        out_shape=jax.ShapeDtypeStruct(x.shape, F32),
        grid=(bsz, n // tm),
        in_specs=[tok(D_MODEL), tok(ATTN_WIDTH), tok(RET_WIDTH), tok(RET_WIDTH), tok(RET_WIDTH),
                  pl.BlockSpec((1, N_MOD, D_MODEL), lambda b, i: (b, 0, 0)),
                  full(g_post_mix), full(g_pre_ffn), full(g_post_ffn), full(seg),
                  full(w_out), full(w_ffn_in), full(w_ffn_out)],
        out_specs=tok(D_MODEL),
        compiler_params=pltpu.CompilerParams(dimension_semantics=("arbitrary", "arbitrary"),
                                             vmem_limit_bytes=VMEM_LIMIT),
        name="merge_outproj_ffn",
    )(x, att, o_f, o_b, gr, mod, g_post_mix, g_pre_ffn, g_post_ffn, seg, w_out, w_ffn_in, w_ffn_out)


def _rope_tables(n_lat):
    rows = n_lat // GRID_W
    n_freq = HEAD_DIM // 4
    inv = ROPE_BASE ** (-jnp.arange(n_freq, dtype=F32) / n_freq)
    ang_row = jnp.arange(rows, dtype=F32)[:, None] * inv
    ang_col = jnp.arange(GRID_W, dtype=F32)[:, None] * inv

    def table(fn, signed):
        per_token = jnp.concatenate(
            [jnp.broadcast_to(fn(ang_row)[:, None, :], (rows, GRID_W, n_freq)),
             jnp.broadcast_to(fn(ang_col)[None, :, :], (rows, GRID_W, n_freq))], axis=-1)
        per_token = per_token.reshape(n_lat, HALF)
        halves = [-per_token, -per_token, per_token, per_token] if signed else [per_token] * 4
        return jnp.concatenate(halves, axis=-1)

    return table(jnp.cos, False), table(jnp.sin, True)


def _lagged_softmax_is_safe(gq, gk, v_ctx, v_lat):
    rounding = 1.03
    bound = (HEAD_DIM * ATTN_SCALE * LOG2E * rounding) * jnp.max(jnp.abs(gq)) * jnp.max(jnp.abs(gk))
    v_max = jnp.maximum(jnp.max(jnp.abs(v_ctx)), jnp.max(jnp.abs(v_lat))).astype(F32)
    n_keys = v_ctx.shape[-1] + v_lat.shape[1] * v_lat.shape[-1]
    log2_numerator = 2.0 * bound + math.log2(n_keys) + jnp.log2(jnp.maximum(v_max, 1.0))
    return (log2_numerator <= F32_MAX_EXP - 2.0).astype(jnp.int32).reshape(1)


def _segment_matrix(width, same):
    i = np.arange(width)
    return jnp.asarray(same(i[:, None], i[None, :]), BF16)


def kernel(x, c, ctx, c_ctx, w_mod, b_mod, g_pre_mix, g_post_mix, g_pre_ffn, g_post_ffn,
           w_in, q_norm_g, k_norm_g, ret_decay_fwd, ret_decay_bwd, w_out, w_ffn_in, w_ffn_out):
    bsz, n_lat, _ = x.shape
    assert w_mod.shape[0] == 1, "single-layer block"
    layer = 0

    w_proj = _projection_weights(w_in[layer]).astype(BF16)
    gq_tab = q_norm_g[layer][_PAIR_DIM].reshape(1, LANES)
    gk_tab = k_norm_g[layer][_PAIR_DIM].reshape(1, LANES)
    seg_pair = _segment_matrix(ATTN_WIDTH, lambda i, j: (i // LANES == j // LANES)
                               & ((i // HALF) % 2 == (j // HALF) % 2))
    seg_std = _segment_matrix(RET_WIDTH, lambda i, j: i // HEAD_DIM == j // HEAD_DIM)
    cos_t, sin_t = _rope_tables(n_lat)

    decay = jnp.stack([ret_decay_fwd[layer], ret_decay_bwd[layer]]).astype(F32)
    lg_head = jnp.broadcast_to(decay[:, :, None, None], (2, RET_HEADS, 1, LANES))
    std_head = np.arange(RET_WIDTH) // HEAD_DIM
    lg_std = decay[:, std_head].reshape(2, N_PAIRS, 1, LANES)
    lg_pair = decay[:, _PAIR8 // HEAD_DIM].reshape(2, N_PAIRS, 1, LANES)

    w_ffn_in_b = w_ffn_in[layer].astype(BF16)
    w_ffn_out_b = w_ffn_out[layer].astype(BF16)
    w_out_b = w_out[layer].astype(BF16)

    c_rows = jnp.concatenate([c, c_ctx[None, :], jnp.zeros((8 - bsz - 1, D_MODEL), c.dtype)], axis=0)
    mod = _mod_call(c_rows, w_mod[layer], b_mod[layer]).reshape(8, N_MOD, D_MODEL)

    g_pre = g_pre_mix[layer].reshape(1, D_MODEL)
    qa, ka_l, va_l, qr, kr, vr, gr = _proj_call(x, mod, g_pre, w_proj, gq_tab, gk_tab, seg_pair,
                                                cos_t, sin_t, latent=True, tm=512)
    ka_c, va_c, kr_c, vr_c = _proj_call(ctx, mod[bsz:bsz + 1], g_pre, w_proj, None, gk_tab, None,
                                        None, None, latent=False, tm=ctx.shape[1])

    tk = va_l.shape[-1]
    lag_ok = _lagged_softmax_is_safe(q_norm_g[layer], k_norm_g[layer], va_c, va_l)
    att = _attn_call(lag_ok, qa, ka_c, va_c, ka_l.reshape(bsz, n_lat // tk, tk, 4 * KV_WIDTH), va_l,
                     tq=512)
    o_f, o_b = _ret_call(lg_head, lg_std, lg_pair, qr, kr, vr, kr_c, vr_c)

    return _merge_ffn_call(x, att, o_f, o_b, gr, mod,
                           g_post_mix[layer].reshape(1, D_MODEL),
                           g_pre_ffn[layer].reshape(1, D_MODEL),
                           g_post_ffn[layer].reshape(1, D_MODEL),
                           seg_std, w_out_b, w_ffn_in_b, w_ffn_out_b, tm=512)
```

```python
import functools
import math

import numpy as np
import jax
import jax.numpy as jnp
from jax import lax
from jax.experimental import pallas as pl
from jax.experimental.pallas import tpu as pltpu

D_MODEL = 1024
HEAD_DIM = 64
HALF = HEAD_DIM // 2
GRID_W = 64
ATTN_HEADS = 8
ATTN_KV_HEADS = 2
RET_HEADS = 8
ATTN_WIDTH = ATTN_HEADS * HEAD_DIM
KV_WIDTH = ATTN_KV_HEADS * HEAD_DIM
RET_WIDTH = RET_HEADS * HEAD_DIM
FFN_HIDDEN = 2816
FFN_CHUNK = 256
N_MOD = 6
RET_CHUNK = 128
RET_CHUNKS_PER_STEP = 8
ROPE_BASE = 10000.0
ATTN_SCALE = HEAD_DIM ** -0.5
EPS = 1e-6
LOG2E = math.log2(math.e)

LANES = 128
MXU_DEPTH = 256
F32_MAX_EXP = 127.0
ONES_ROWS = 16
V_ROWS = HEAD_DIM + ONES_ROWS
VMEM_LIMIT = 56 * 1024 * 1024

F32 = jnp.float32
BF16 = jnp.bfloat16

_QA0 = 0
_KA0 = ATTN_WIDTH
_VA0 = ATTN_WIDTH + KV_WIDTH
_QR0 = ATTN_WIDTH + 2 * KV_WIDTH
_KR0 = _QR0 + RET_WIDTH
_VR0 = _KR0 + RET_WIDTH
_GR0 = _VR0 + RET_WIDTH


def _pair_perm(n_heads):
    idx = []
    for p in range(n_heads // 2):
        for seg in range(4):
            head = 2 * p + (seg % 2)
            lo = (seg // 2) * HALF
            idx.extend(head * HEAD_DIM + lo + d for d in range(HALF))
    return np.asarray(idx, np.int32)


_PAIR8 = _pair_perm(8)


def _pair_layout(w):
    rows, width = w.shape
    return (w.reshape(rows, width // LANES, 2, 2, HALF).transpose(0, 1, 3, 2, 4)
            .reshape(rows, width))


def _projection_weights(w):
    rows = w.shape[0]
    ka = w[:, _KA0:_KA0 + KV_WIDTH].reshape(rows, ATTN_KV_HEADS, 2, 1, HALF)
    ka = jnp.broadcast_to(ka, (rows, ATTN_KV_HEADS, 2, 2, HALF)).reshape(rows, 2 * KV_WIDTH)
    va = w[:, _VA0:_VA0 + KV_WIDTH]
    groups = [("qa", _pair_layout(w[:, _QA0:_QA0 + ATTN_WIDTH])), ("ka", ka), ("va", va),
              ("qr", _pair_layout(w[:, _QR0:_QR0 + RET_WIDTH])),
              ("kr", _pair_layout(w[:, _KR0:_KR0 + RET_WIDTH])),
              ("vr", w[:, _VR0:_VR0 + RET_WIDTH]), ("gr", w[:, _GR0:_GR0 + RET_WIDTH])]
    return jnp.concatenate([g for _, g in groups], axis=1)


_PROJ_COLS = {"qa": (0, ATTN_WIDTH), "ka": (ATTN_WIDTH, 2 * KV_WIDTH),
              "va": (ATTN_WIDTH + 2 * KV_WIDTH, KV_WIDTH)}
for _i, _name in enumerate(("qr", "kr", "vr", "gr")):
    _PROJ_COLS[_name] = (ATTN_WIDTH + 3 * KV_WIDTH + _i * RET_WIDTH, RET_WIDTH)
_PAIR_DIM = (_PAIR8[:LANES] % HEAD_DIM).astype(np.int32)


def _dot(a, b):
    return jnp.dot(a, b, preferred_element_type=F32)


def _dot_nt(a, b):
    return lax.dot_general(a, b, (((1,), (1,)), ((), ())), preferred_element_type=F32)


def _dot_tn(a, b):
    return lax.dot_general(a, b, (((0,), (0,)), ((), ())), preferred_element_type=F32)


def _head_sums(a, same_head_bf16, exact=False):
    hi = a.astype(BF16)
    out = _dot(hi, same_head_bf16)
    if exact:
        out = out + _dot((a - hi.astype(F32)).astype(BF16), same_head_bf16)
    return out


def _silu(x):
    return x * (1.0 / (1.0 + jnp.exp(-x)))


def _lane_iota(shape):
    return lax.broadcasted_iota(jnp.int32, shape, len(shape) - 1)


def _mod_kernel(c_ref, w_ref, b_ref, o_ref):
    h = _silu(c_ref[...]).astype(BF16)
    o_ref[...] = _dot(h, w_ref[...].astype(BF16)) + b_ref[...]


def _mod_call(c_rows, w_mod, b_mod):
    rows = c_rows.shape[0]
    n_out = w_mod.shape[1]
    tn = D_MODEL
    return pl.pallas_call(
        _mod_kernel,
        out_shape=jax.ShapeDtypeStruct((rows, n_out), F32),
        grid=(n_out // tn,),
        in_specs=[
            pl.BlockSpec((rows, D_MODEL), lambda j: (0, 0)),
            pl.BlockSpec((D_MODEL, tn), lambda j: (0, j)),
            pl.BlockSpec((1, tn), lambda j: (0, j)),
        ],
        out_specs=pl.BlockSpec((rows, tn), lambda j: (0, j)),
        compiler_params=pltpu.CompilerParams(dimension_semantics=("arbitrary",),
                                             vmem_limit_bytes=VMEM_LIMIT),
        name="adaln_mod",
    )(c_rows, w_mod, b_mod.reshape(1, n_out))


def _rope(y, cos_t, sin_t):
    blocks = []
    for p in range(y.shape[1] // LANES):
        yp = y[:, p * LANES:(p + 1) * LANES]
        blocks.append(yp * cos_t + pltpu.roll(yp, LANES // 2, 1) * sin_t)
    return blocks[0] if len(blocks) == 1 else jnp.concatenate(blocks, axis=1)


def _proj_kernel(*refs, latent):
    if latent:
        (x_ref, mod_ref, g_ref, w_ref, gq_ref, gk_ref, seg_ref, cos_ref, sin_ref,
         qa_ref, ka_ref, va_ref, qr_ref, kr_ref, vr_ref, gr_ref) = refs
    else:
        (x_ref, mod_ref, g_ref, w_ref, gk_ref,
         ka_ref, va_ref, kr_ref, vr_ref) = refs

    x = x_ref[0]
    ms = jnp.mean(x * x, axis=-1, keepdims=True)
    shift = mod_ref[0, 0:1, :]
    scale = mod_ref[0, 1:2, :]
    h = (x * lax.rsqrt(ms + EPS) * g_ref[...]) * (1.0 + scale) + shift
    hb = h.astype(BF16)

    def proj(name):
        start, width = _PROJ_COLS[name]
        return _dot(hb, w_ref[:, start:start + width])

    if latent:
        cos_t = cos_ref[...]
        sin_t = sin_ref[...]
        rope = lambda y: _rope(y, cos_t, sin_t)
        yq = proj("qa")
        ssq = _head_sums(yq * yq, seg_ref[...])
        gq = jnp.concatenate([gq_ref[...]] * (ATTN_WIDTH // LANES), axis=1)
        qn = yq * lax.rsqrt(ssq * (1.0 / HEAD_DIM) + EPS) * gq
        qa_ref[0] = (rope(qn) * (ATTN_SCALE * LOG2E)).T.astype(BF16)
    else:
        rope = lambda y: y

    yk = proj("ka")
    slot_b = (_lane_iota((yk.shape[0], LANES)) // HALF) % 2 == 1
    kblocks = []
    for kv in range(ATTN_KV_HEADS):
        ykv = yk[:, kv * LANES:(kv + 1) * LANES]
        ssq = 0.5 * jnp.sum(ykv * ykv, axis=-1, keepdims=True)
        kn = rope(ykv * lax.rsqrt(ssq * (1.0 / HEAD_DIM) + EPS) * gk_ref[...])
        kblocks += [jnp.where(slot_b, 0.0, kn), jnp.where(slot_b, kn, 0.0)]
    ka_ref[0] = jnp.concatenate(kblocks, axis=1).astype(BF16)

    yv_t = proj("va").T
    ones = jnp.ones((ONES_ROWS, yv_t.shape[1]), F32)
    va_ref[0, 0] = jnp.concatenate(
        [blk for kv in range(ATTN_KV_HEADS)
         for blk in (yv_t[kv * HEAD_DIM:(kv + 1) * HEAD_DIM], ones)], axis=0).astype(BF16)

    if latent:
        qr_ref[0] = rope(proj("qr")).astype(BF16)
    kr_ref[0] = rope(proj("kr") * ATTN_SCALE).astype(BF16)
    vr_ref[0] = proj("vr").astype(BF16)
    if latent:
        gr_ref[0] = proj("gr")


def _proj_call(x, mod, g_pre, w, gq_tab, gk_tab, seg, cos_t, sin_t, *, latent, tm):
    bsz, n, _ = x.shape
    wcols = w.shape[1]
    tok = lambda width: pl.BlockSpec((1, tm, width), lambda b, i: (b, i, 0))
    full = lambda shape: pl.BlockSpec(shape, lambda b, i: (0,) * len(shape))
    mod_spec = pl.BlockSpec((1, N_MOD, D_MODEL), (lambda b, i: (b, 0, 0)) if latent
                            else (lambda b, i: (0, 0, 0)))
    if latent:
        in_specs = [tok(D_MODEL), mod_spec, full((1, D_MODEL)), full((D_MODEL, wcols)),
                    full((1, LANES)), full((1, LANES)), full((ATTN_WIDTH, ATTN_WIDTH)),
                    pl.BlockSpec((tm, LANES), lambda b, i: (i, 0)),
                    pl.BlockSpec((tm, LANES), lambda b, i: (i, 0))]
        args = (x, mod, g_pre, w, gq_tab, gk_tab, seg, cos_t, sin_t)
    else:
        in_specs = [tok(D_MODEL), mod_spec, full((1, D_MODEL)), full((D_MODEL, wcols)),
                    full((1, LANES))]
        args = (x, mod, g_pre, w, gk_tab)
    token_major = lambda width, dt: (jax.ShapeDtypeStruct((bsz, n, width), dt), tok(width))
    q_t = (jax.ShapeDtypeStruct((bsz, ATTN_WIDTH, n), BF16),
           pl.BlockSpec((1, ATTN_WIDTH, tm), lambda b, i: (b, 0, i)))
    k_ab = token_major(4 * KV_WIDTH, BF16)
    v_t = (jax.ShapeDtypeStruct((bsz, n // tm, ATTN_KV_HEADS * V_ROWS, tm), BF16),
           pl.BlockSpec((1, 1, ATTN_KV_HEADS * V_ROWS, tm), lambda b, i: (b, i, 0, 0)))
    ret = token_major(RET_WIDTH, BF16)
    outs = ([q_t, k_ab, v_t, ret, ret, ret, token_major(RET_WIDTH, F32)] if latent
            else [k_ab, v_t, ret, ret])
    return pl.pallas_call(
        functools.partial(_proj_kernel, latent=latent),
        out_shape=[o[0] for o in outs],
        grid=(bsz, n // tm),
        in_specs=in_specs,
        out_specs=[o[1] for o in outs],
        compiler_params=pltpu.CompilerParams(dimension_semantics=("arbitrary", "arbitrary"),
                                             vmem_limit_bytes=VMEM_LIMIT),
        name="in_proj_latent" if latent else "in_proj_ctx",
    )(*args)


LOOKAHEAD = 2
TILES_PER_TRIP = 8
ITEM_KEYS = 512
DRAIN_ITEMS = 0
Q_SPLITS = 2


def _attn_kernel(lag_ok_ref, qt_ref, kc_ref, vc_ref, kl_ref, vl_ref, o_ref, m_ref, acc_ref, s_ref):
    n_heads = 2 * (qt_ref.shape[1] // LANES)
    tq = qt_ref.shape[2] // Q_SPLITS
    n_streams = n_heads * Q_SPLITS

    m_ref[...] = jnp.full(m_ref.shape, -jnp.inf, F32)
    acc_ref[...] = jnp.zeros(acc_ref.shape, F32)

    n_tiles = kl_ref.shape[1]

    def scores(k_ref, item):
        sub, stream = divmod(item, n_streams)
        qb, h = divmod(stream, n_heads)
        q_t = qt_ref[0, (h // 2) * LANES:(h // 2 + 1) * LANES, qb * tq:(qb + 1) * tq]
        keys = min(ITEM_KEYS, k_ref.shape[0])
        half = keys // 2
        lo = sub * keys
        cols = slice((h % 2) * LANES, (h % 2 + 1) * LANES)
        return (_dot(k_ref[lo:lo + half, cols], q_t), _dot(k_ref[lo + half:lo + keys, cols], q_t))

    def accumulate(item, s, v_ref, lagged):
        sub, h = divmod(item, n_streams)
        half = s[0].shape[0]
        lo = sub * 2 * half
        m_old = m_ref[h]
        m_new = jnp.maximum(m_old, jnp.maximum(jnp.max(s[0], axis=0, keepdims=True),
                                               jnp.max(s[1], axis=0, keepdims=True)))
        alpha = jnp.exp2(m_old - m_new)
        shift = m_old if lagged else m_new
        p0 = jnp.exp2(s[0] - shift).astype(BF16)
        p1 = jnp.exp2(s[1] - shift).astype(BF16)
        if 2 * half <= MXU_DEPTH:
            pv = _dot(v_ref[:, lo:lo + 2 * half], jnp.concatenate([p0, p1], axis=0))
        else:
            pv = _dot(v_ref[:, lo:lo + half], p0) + _dot(v_ref[:, lo + half:lo + 2 * half], p1)
        acc_ref[h] = alpha * (acc_ref[h] + pv) if lagged else alpha * acc_ref[h] + pv
        m_ref[h] = m_new

    def run_items(s_ahead, tiles, after, lagged):
        per_tile = [n_streams * max(1, k_ref.shape[0] // ITEM_KEYS) for k_ref, _ in tiles]
        refs = [(k_ref, v_ref, i) for (k_ref, v_ref), n in zip(tiles, per_tile) for i in range(n)]
        refs_after = [(after, None, i) for i in range(LOOKAHEAD)]
        total = len(refs)
        requests = [[i + LOOKAHEAD] for i in range(total)]
        drain = min(DRAIN_ITEMS, total // 2)
        for i in range(total - drain, total):
            requests[i - drain] += requests[i]
            requests[i] = []
        pending = dict(enumerate(s_ahead))
        for i, (k_ref, v_ref, item) in enumerate(refs):
            for r in requests[i]:
                rk, _, ritem = (refs + refs_after)[r]
                pending[r] = scores(rk, ritem)
            accumulate(item, pending.pop(i), v_ref, lagged)
        return [pending[total + j] for j in range(LOOKAHEAD)]

    def save(s_ahead):
        for i, s in enumerate(s_ahead):
            s_ref[i, 0] = s[0]
            s_ref[i, 1] = s[1]

    ctx_k = kc_ref.at[0]
    save(run_items([scores(ctx_k, i) for i in range(LOOKAHEAD)], [(ctx_k, vc_ref.at[0, 0])],
                   kl_ref.at[0, 0], lagged=False))

    def latent_tiles(lagged):
        def body(jj, carry):
            j0 = jj * TILES_PER_TRIP
            tiles = [(kl_ref.at[0, j0 + u], vl_ref.at[0, j0 + u]) for u in range(TILES_PER_TRIP)]
            after = kl_ref.at[0, jnp.minimum(j0 + TILES_PER_TRIP, n_tiles - 1)]
            save(run_items([(s_ref[i, 0], s_ref[i, 1]) for i in range(LOOKAHEAD)], tiles, after,
                           lagged))
            return carry

        lax.fori_loop(0, n_tiles // TILES_PER_TRIP, body, 0)

    assert n_tiles % TILES_PER_TRIP == 0
    lag_ok = lag_ok_ref[0] != 0
    pl.when(lag_ok)(lambda: latent_tiles(True))
    pl.when(jnp.logical_not(lag_ok))(lambda: latent_tiles(False))

    for qb in range(Q_SPLITS):
        heads = []
        for h in range(n_heads):
            acc = acc_ref[qb * n_heads + h]
            heads.append(acc[:HEAD_DIM] / acc[HEAD_DIM:HEAD_DIM + 1])
        o_ref[0, qb * tq:(qb + 1) * tq, :] = jnp.concatenate(heads, axis=0).T.astype(o_ref.dtype)


def _attn_call(lag_ok, qa_t, ka_c, va_c, ka_l, va_l, *, tq):
    bsz, _, n = qa_t.shape
    n_ctx = ka_c.shape[1]
    n_tiles, tk = ka_l.shape[1], ka_l.shape[2]
    gw = ATTN_WIDTH // ATTN_KV_HEADS
    n_heads = ATTN_HEADS // ATTN_KV_HEADS
    return pl.pallas_call(
        _attn_kernel,
        out_shape=jax.ShapeDtypeStruct((bsz, n, ATTN_WIDTH), BF16),
        grid=(bsz, ATTN_KV_HEADS, n // tq),
        in_specs=[pl.BlockSpec(memory_space=pltpu.SMEM),
                  pl.BlockSpec((1, gw, tq), lambda b, g, i: (b, g, i)),
                  pl.BlockSpec((1, n_ctx, 2 * LANES), lambda b, g, i: (b, 0, g)),
                  pl.BlockSpec((1, 1, V_ROWS, n_ctx), lambda b, g, i: (b, 0, g, 0)),
                  pl.BlockSpec((1, n_tiles, tk, 2 * LANES), lambda b, g, i: (b, 0, 0, g)),
                  pl.BlockSpec((1, n_tiles, V_ROWS, tk), lambda b, g, i: (b, 0, g, 0))],
        out_specs=pl.BlockSpec((1, tq, gw), lambda b, g, i: (b, i, g)),
        scratch_shapes=[pltpu.VMEM((n_heads * Q_SPLITS, 1, tq // Q_SPLITS), F32),
                        pltpu.VMEM((n_heads * Q_SPLITS, V_ROWS, tq // Q_SPLITS), F32),
                        pltpu.VMEM((LOOKAHEAD, 2, ITEM_KEYS // 2, tq // Q_SPLITS), F32)],
        compiler_params=pltpu.CompilerParams(
            dimension_semantics=("arbitrary", "arbitrary", "arbitrary"),
            vmem_limit_bytes=VMEM_LIMIT),
        name="gqa_flash_attention",
    )(lag_ok, qa_t, ka_c, va_c, ka_l, va_l)


N_PAIRS = RET_HEADS // 2


def _ret_kernel(lg_head_ref, lg_std_ref, lg_pair_ref,
                qm_ref, km_ref, vm_ref, qb_ref, kb_ref, vb_ref, kc_ref, vc_ref,
                om_ref, ob_ref,
                dsum_ref, qdec_ref, kdec_ref, cdec_ref, state_ref):
    C = RET_CHUNK
    n_ctx = kc_ref.shape[1]
    first_call = (pl.program_id(0) == 0) & (pl.program_id(1) == 0)

    std_a = _lane_iota((C, LANES)) < HEAD_DIM
    pair_a = (_lane_iota((C, LANES)) // HALF) % 2 == 0
    row_head = (lax.broadcasted_iota(jnp.int32, (LANES, LANES), 0) // HALF) % 2
    col_head = lax.broadcasted_iota(jnp.int32, (LANES, LANES), 1) // HEAD_DIM
    same_head = row_head == col_head

    @pl.when(first_call)
    def _tables():
        row = lax.broadcasted_iota(jnp.int32, (C, C), 0).astype(F32)
        colv = lax.broadcasted_iota(jnp.int32, (C, C), 1).astype(F32)
        pos = row[:, :LANES]
        for h in range(RET_HEADS):
            lg_f = -jnp.exp(lg_head_ref[0, h])
            lg_b = -jnp.exp(lg_head_ref[1, h])
            dsum_ref[h] = (jnp.where(row >= colv, jnp.exp(lg_f * jnp.maximum(row - colv, 0.0)), 0.0)
                           + jnp.where(colv >= row, jnp.exp(lg_b * jnp.maximum(colv - row, 0.0)), 0.0))
        for d in range(2):
            for p in range(N_PAIRS):
                lg_std = -jnp.exp(lg_std_ref[d, p])
                lg_pair = -jnp.exp(lg_pair_ref[d, p])
                q_steps = (pos + 1.0) if d == 0 else (C - pos)
                k_steps = (C - 1.0 - pos) if d == 0 else pos
                qdec_ref[d, p] = jnp.exp(lg_std * q_steps)
                kdec_ref[d, p] = jnp.exp(lg_pair * k_steps)
                cdec_ref[d, p] = jnp.exp(lg_std * float(C))

    @pl.when(pl.program_id(1) == 0)
    def _seed_states():
        cpos = lax.broadcasted_iota(jnp.int32, (n_ctx, LANES), 0).astype(F32)
        for d in range(2):
            steps = (n_ctx - 1.0 - cpos) if d == 0 else cpos
            for p in range(N_PAIRS):
                lg_pair = -jnp.exp(lg_pair_ref[d, p])
                kc = kc_ref[0, :, p * LANES:(p + 1) * LANES].astype(F32) * jnp.exp(lg_pair * steps)
                vc = vc_ref[0, :, p * LANES:(p + 1) * LANES]
                state_ref[d, p] = jnp.where(same_head, _dot_tn(kc.astype(BF16), vc), 0.0)

    def decayed(x, dec):
        return (x.astype(F32) * dec).astype(BF16)

    for u in range(RET_CHUNKS_PER_STEP):
        rows_m = slice(u * C, (u + 1) * C)
        rows_b = slice((RET_CHUNKS_PER_STEP - 1 - u) * C, (RET_CHUNKS_PER_STEP - u) * C)

        stage1 = []
        for p in range(N_PAIRS):
            sl = slice(p * LANES, (p + 1) * LANES)
            q, k, v = qm_ref[0, rows_m, sl], km_ref[0, rows_m, sl], vm_ref[0, rows_m, sl]
            q2, k2, v2 = qb_ref[0, rows_b, sl], kb_ref[0, rows_b, sl], vb_ref[0, rows_b, sl]
            zk = jnp.zeros_like(k)
            k_split = jnp.concatenate([jnp.where(pair_a, k, zk), jnp.where(pair_a, zk, k)], axis=0)
            s = _dot_nt(q, k_split)
            cross_f = _dot(q, state_ref[0, p].astype(BF16)) * qdec_ref[0, p]
            cross_b = _dot(q2, state_ref[1, p].astype(BF16)) * qdec_ref[1, p]
            kv_f = _dot_tn(decayed(k, kdec_ref[0, p]), v)
            kv_b = _dot_tn(decayed(k2, kdec_ref[1, p]), v2)
            stage1.append((s, cross_f, cross_b, kv_f, kv_b, v))

        for p, (s, cross_f, cross_b, kv_f, kv_b, v) in enumerate(stage1):
            sl = slice(p * LANES, (p + 1) * LANES)
            w = jnp.concatenate([s[:, :C] * dsum_ref[2 * p], s[:, C:] * dsum_ref[2 * p + 1]], axis=1)
            zv = jnp.zeros_like(v)
            v_split = jnp.concatenate([jnp.where(std_a, v, zv), jnp.where(std_a, zv, v)], axis=0)
            om_ref[0, rows_m, sl] = (_dot(w.astype(BF16), v_split) + cross_f).astype(om_ref.dtype)
            ob_ref[0, rows_b, sl] = cross_b.astype(ob_ref.dtype)
            state_ref[0, p] = state_ref[0, p] * cdec_ref[0, p] + jnp.where(same_head, kv_f, 0.0)
            state_ref[1, p] = state_ref[1, p] * cdec_ref[1, p] + jnp.where(same_head, kv_b, 0.0)


def _ret_call(lg_head, lg_std, lg_pair, qr, kr, vr, kr_c, vr_c):
    bsz, n, _ = qr.shape
    n_ctx = kr_c.shape[1]
    C = RET_CHUNK
    rows = C * RET_CHUNKS_PER_STEP
    nc = n // rows
    fwd = pl.BlockSpec((1, rows, RET_WIDTH), lambda b, c: (b, c, 0))
    bwd = pl.BlockSpec((1, rows, RET_WIDTH), lambda b, c: (b, nc - 1 - c, 0))
    ctx = pl.BlockSpec((1, n_ctx, RET_WIDTH), lambda b, c: (b, 0, 0))
    full = lambda a: pl.BlockSpec(a.shape, lambda b, c: (0,) * a.ndim)
    return pl.pallas_call(
        _ret_kernel,
        out_shape=[jax.ShapeDtypeStruct((bsz, n, RET_WIDTH), BF16)] * 2,
        grid=(bsz, nc),
        in_specs=[full(lg_head), full(lg_std), full(lg_pair), fwd, fwd, fwd, bwd, bwd, bwd, ctx, ctx],
        out_specs=[fwd, bwd],
        scratch_shapes=[pltpu.VMEM((RET_HEADS, C, C), F32),
                        pltpu.VMEM((2, N_PAIRS, C, LANES), F32),
                        pltpu.VMEM((2, N_PAIRS, C, LANES), F32),
                        pltpu.VMEM((2, N_PAIRS, 1, LANES), F32),
                        pltpu.VMEM((2, N_PAIRS, LANES, LANES), F32)],
        compiler_params=pltpu.CompilerParams(dimension_semantics=("arbitrary", "arbitrary"),
                                             vmem_limit_bytes=VMEM_LIMIT),
        name="bidir_retention",
    )(lg_head, lg_std, lg_pair, qr, kr, vr, qr, kr, vr, kr_c, vr_c)


def _merge_ffn_kernel(x_ref, att_ref, of_ref, ob_ref, gr_ref, mod_ref, gpm_ref, gpf_ref, gqf_ref,
                      seg_ref, wo_ref, wi_ref, w2_ref, out_ref):
    x = x_ref[0]
    gate_m = mod_ref[0, 2:3, :]
    shift_f = mod_ref[0, 3:4, :]
    scale_f = mod_ref[0, 4:5, :]
    gate_f = mod_ref[0, 5:6, :]

    ret = of_ref[0].astype(F32) + ob_ref[0].astype(F32)
    seg = seg_ref[...]
    mu = _head_sums(ret, seg, exact=True) * (1.0 / HEAD_DIM)
    cen = ret - mu
    var = _head_sums(cen * cen, seg) * (1.0 / HEAD_DIM)
    ret_n = cen * lax.rsqrt(var + EPS) * _silu(gr_ref[0])

    mix = (_dot(att_ref[0], wo_ref[:ATTN_WIDTH, :])
           + _dot(ret_n.astype(BF16), wo_ref[ATTN_WIDTH:, :]))
    ms = jnp.mean(mix * mix, axis=-1, keepdims=True)
    x1 = x + gate_m * (mix * lax.rsqrt(ms + EPS) * gpm_ref[...])

    ms1 = jnp.mean(x1 * x1, axis=-1, keepdims=True)
    hf = ((x1 * lax.rsqrt(ms1 + EPS) * gpf_ref[...]) * (1.0 + scale_f) + shift_f).astype(BF16)

    acc = jnp.zeros(x.shape, F32)
    for j in range(FFN_HIDDEN // FFN_CHUNK):
        lo = j * FFN_CHUNK
        a = _dot(hf, wi_ref[:, lo:lo + FFN_CHUNK])
        b = _dot(hf, wi_ref[:, FFN_HIDDEN + lo:FFN_HIDDEN + lo + FFN_CHUNK])
        acc = acc + _dot((_silu(a) * b).astype(BF16), w2_ref[lo:lo + FFN_CHUNK, :])
    ms2 = jnp.mean(acc * acc, axis=-1, keepdims=True)
    out_ref[0] = x1 + gate_f * (acc * lax.rsqrt(ms2 + EPS) * gqf_ref[...])


def _merge_ffn_call(x, att, o_f, o_b, gr, mod, g_post_mix, g_pre_ffn, g_post_ffn, seg, w_out,
                    w_ffn_in, w_ffn_out, *, tm):
    bsz, n, _ = x.shape
    tok = lambda width: pl.BlockSpec((1, tm, width), lambda b, i: (b, i, 0))
    full = lambda a: pl.BlockSpec(a.shape, lambda b, i: (0,) * a.ndim,
                                  pipeline_mode=pl.Buffered(1))
    return pl.pallas_call(
        _merge_ffn_kernel,
        out_shape=jax.ShapeDtypeStruct(x.shape, F32),
        grid=(bsz, n // tm),
        in_specs=[tok(D_MODEL), tok(ATTN_WIDTH), tok(RET_WIDTH), tok(RET_WIDTH), tok(RET_WIDTH),
                  pl.BlockSpec((1, N_MOD, D_MODEL), lambda b, i: (b, 0, 0)),
                  full(g_post_mix), full(g_pre_ffn), full(g_post_ffn), full(seg),
                  full(w_out), full(w_ffn_in), full(w_ffn_out)],
        out_specs=tok(D_MODEL),
        compiler_params=pltpu.CompilerParams(dimension_semantics=("arbitrary", "arbitrary"),
                                             vmem_limit_bytes=VMEM_LIMIT),
        name="merge_outproj_ffn",
    )(x, att, o_f, o_b, gr, mod, g_post_mix, g_pre_ffn, g_post_ffn, seg, w_out, w_ffn_in, w_ffn_out)


def _rope_tables(n_lat):
    rows = n_lat // GRID_W
    n_freq = HEAD_DIM // 4
    inv = ROPE_BASE ** (-jnp.arange(n_freq, dtype=F32) / n_freq)
    ang_row = jnp.arange(rows, dtype=F32)[:, None] * inv
    ang_col = jnp.arange(GRID_W, dtype=F32)[:, None] * inv

    def table(fn, signed):
        per_token = jnp.concatenate(
            [jnp.broadcast_to(fn(ang_row)[:, None, :], (rows, GRID_W, n_freq)),
             jnp.broadcast_to(fn(ang_col)[None, :, :], (rows, GRID_W, n_freq))], axis=-1)
        per_token = per_token.reshape(n_lat, HALF)
        halves = [-per_token, -per_token, per_token, per_token] if signed else [per_token] * 4
        return jnp.concatenate(halves, axis=-1)

    return table(jnp.cos, False), table(jnp.sin, True)


def _lagged_softmax_is_safe(gq, gk, v_ctx, v_lat):
    rounding = 1.03
    bound = (HEAD_DIM * ATTN_SCALE * LOG2E * rounding) * jnp.max(jnp.abs(gq)) * jnp.max(jnp.abs(gk))
    v_max = jnp.maximum(jnp.max(jnp.abs(v_ctx)), jnp.max(jnp.abs(v_lat))).astype(F32)
    n_keys = v_ctx.shape[-1] + v_lat.shape[1] * v_lat.shape[-1]
    log2_numerator = 2.0 * bound + math.log2(n_keys) + jnp.log2(jnp.maximum(v_max, 1.0))
    return (log2_numerator <= F32_MAX_EXP - 2.0).astype(jnp.int32).reshape(1)


def _segment_matrix(width, same):
    i = np.arange(width)
    return jnp.asarray(same(i[:, None], i[None, :]), BF16)


def kernel(x, c, ctx, c_ctx, w_mod, b_mod, g_pre_mix, g_post_mix, g_pre_ffn, g_post_ffn,
           w_in, q_norm_g, k_norm_g, ret_decay_fwd, ret_decay_bwd, w_out, w_ffn_in, w_ffn_out):
    bsz, n_lat, _ = x.shape
    assert w_mod.shape[0] == 1, "single-layer block"
    layer = 0

    w_proj = _projection_weights(w_in[layer]).astype(BF16)
    gq_tab = q_norm_g[layer][_PAIR_DIM].reshape(1, LANES)
    gk_tab = k_norm_g[layer][_PAIR_DIM].reshape(1, LANES)
    seg_pair = _segment_matrix(ATTN_WIDTH, lambda i, j: (i // LANES == j // LANES)
                               & ((i // HALF) % 2 == (j // HALF) % 2))
    seg_std = _segment_matrix(RET_WIDTH, lambda i, j: i // HEAD_DIM == j // HEAD_DIM)
    cos_t, sin_t = _rope_tables(n_lat)

    decay = jnp.stack([ret_decay_fwd[layer], ret_decay_bwd[layer]]).astype(F32)
    lg_head = jnp.broadcast_to(decay[:, :, None, None], (2, RET_HEADS, 1, LANES))
    std_head = np.arange(RET_WIDTH) // HEAD_DIM
    lg_std = decay[:, std_head].reshape(2, N_PAIRS, 1, LANES)
    lg_pair = decay[:, _PAIR8 // HEAD_DIM].reshape(2, N_PAIRS, 1, LANES)

    w_ffn_in_b = w_ffn_in[layer].astype(BF16)
    w_ffn_out_b = w_ffn_out[layer].astype(BF16)
    w_out_b = w_out[layer].astype(BF16)

    c_rows = jnp.concatenate([c, c_ctx[None, :], jnp.zeros((8 - bsz - 1, D_MODEL), c.dtype)], axis=0)
    mod = _mod_call(c_rows, w_mod[layer], b_mod[layer]).reshape(8, N_MOD, D_MODEL)

    g_pre = g_pre_mix[layer].reshape(1, D_MODEL)
    qa, ka_l, va_l, qr, kr, vr, gr = _proj_call(x, mod, g_pre, w_proj, gq_tab, gk_tab, seg_pair,
                                                cos_t, sin_t, latent=True, tm=512)
    ka_c, va_c, kr_c, vr_c = _proj_call(ctx, mod[bsz:bsz + 1], g_pre, w_proj, None, gk_tab, None,
                                        None, None, latent=False, tm=ctx.shape[1])

    tk = va_l.shape[-1]
    lag_ok = _lagged_softmax_is_safe(q_norm_g[layer], k_norm_g[layer], va_c, va_l)
    att = _attn_call(lag_ok, qa, ka_c, va_c, ka_l.reshape(bsz, n_lat // tk, tk, 4 * KV_WIDTH), va_l,
                     tq=512)
    o_f, o_b = _ret_call(lg_head, lg_std, lg_pair, qr, kr, vr, kr_c, vr_c)

    return _merge_ffn_call(x, att, o_f, o_b, gr, mod,
                           g_post_mix[layer].reshape(1, D_MODEL),
                           g_pre_ffn[layer].reshape(1, D_MODEL),
                           g_post_ffn[layer].reshape(1, D_MODEL),
                           seg_std, w_out_b, w_ffn_in_b, w_ffn_out_b, tm=512)
```
